```python
import math
import jax, jax.numpy as jnp
from jax import lax
import numpy as np

D_MODEL = 2048
BATCH = 2
SEQ = 4096
DEPTH = 1

MIX_WIDTH = D_MODEL
ATTN_WIDTH = MIX_WIDTH // 2
SSM_WIDTH = MIX_WIDTH - ATTN_WIDTH
ATTN_HEADS = 8
ATTN_HEAD_DIM = ATTN_WIDTH // (2 * ATTN_HEADS)
VALUE_DIM = 2 * ATTN_HEAD_DIM
SSM_GROUP = 16
SSM_GROUPS = SSM_WIDTH // SSM_GROUP
SSM_STATE = 64
D_FF = 5632
CONV_WIDTH = 3
Q_BLOCK = 128
EPS = 1e-6
DT_MIN = 0.001
DT_MAX = 0.1
PROJ_IN = 3 * ATTN_WIDTH + SSM_WIDTH

kernel_name = "hymba_diffattn_s5_convffn_block"


def rms_norm(x, g):
    xf = x.astype(jnp.float32)
    y = xf * lax.rsqrt(jnp.mean(xf * xf, axis=-1, keepdims=True) + EPS)
    return (y * g.astype(jnp.float32)).astype(x.dtype)


def modulate(h, shift, scale):
    return h * (1.0 + scale[:, None, :]) + shift[:, None, :]


def alibi_slopes(n_heads):
    return jnp.asarray([2.0 ** (-8.0 * (i + 1) / n_heads) for i in range(n_heads)], jnp.float32)


def diff_attention(q1, q2, k1, k2, v, lam):
    B, H, L, d = q1.shape
    n_blocks = L // Q_BLOCK
    slopes = alibi_slopes(H)
    key_pos = jnp.arange(L)
    scale = d ** -0.5

    def block(i):
        start = i * Q_BLOCK
        qb1 = lax.dynamic_slice_in_dim(q1, start, Q_BLOCK, axis=2)
        qb2 = lax.dynamic_slice_in_dim(q2, start, Q_BLOCK, axis=2)
        q_pos = start + jnp.arange(Q_BLOCK)
        dist = (q_pos[:, None] - key_pos[None, :]).astype(jnp.float32)
        bias = jnp.where(dist >= 0, -slopes[:, None, None] * dist, -jnp.inf)
        s1 = jnp.einsum('bhqd,bhkd->bhqk', qb1, k1).astype(jnp.float32) * scale + bias
        s2 = jnp.einsum('bhqd,bhkd->bhqk', qb2, k2).astype(jnp.float32) * scale + bias
        w = jax.nn.softmax(s1, axis=-1) - lam * jax.nn.softmax(s2, axis=-1)
        return jnp.einsum('bhqk,bhkv->bhqv', w.astype(v.dtype), v)

    out = lax.map(block, jnp.arange(n_blocks))
    return out.transpose(1, 0, 3, 2, 4).reshape(B, L, H, v.shape[-1])


def complex_affine_combine(e1, e2):
    ar1, ai1, br1, bi1 = e1
    ar2, ai2, br2, bi2 = e2
    ar = ar2 * ar1 - ai2 * ai1
    ai = ar2 * ai1 + ai2 * ar1
    br = ar2 * br1 - ai2 * bi1 + br2
    bi = ar2 * bi1 + ai2 * br1 + bi2
    return (ar, ai, br, bi)


def s5_ssm(u, a_re, a_im, log_dt, b_re, b_im, c_re, c_im, d_skip):
    B, L, _ = u.shape
    f32 = jnp.float32
    uf = u.astype(f32).reshape(B, L, SSM_GROUPS, SSM_GROUP)
    dt = jnp.exp(log_dt.astype(f32))[:, None]
    ar, ai = a_re.astype(f32), a_im.astype(f32)
    mag = jnp.exp(dt * ar)
    abar_re, abar_im = mag * jnp.cos(dt * ai), mag * jnp.sin(dt * ai)
    den = ar * ar + ai * ai
    nr, ni = abar_re - 1.0, abar_im
    coef_re = (nr * ar + ni * ai) / den
    coef_im = (ni * ar - nr * ai) / den
    br, bi = b_re.astype(f32), b_im.astype(f32)
    bbar_re = coef_re[..., None] * br - coef_im[..., None] * bi
    bbar_im = coef_re[..., None] * bi + coef_im[..., None] * br
    bu_re = jnp.einsum('blgc,gpc->blgp', uf, bbar_re)
    bu_im = jnp.einsum('blgc,gpc->blgp', uf, bbar_im)
    a_re_b = jnp.broadcast_to(abar_re, bu_re.shape)
    a_im_b = jnp.broadcast_to(abar_im, bu_im.shape)
    _, _, x_re, x_im = lax.associative_scan(
        complex_affine_combine, (a_re_b, a_im_b, bu_re, bu_im), axis=1)
    y = (jnp.einsum('blgp,gcp->blgc', x_re, c_re.astype(f32))
         - jnp.einsum('blgp,gcp->blgc', x_im, c_im.astype(f32))
         + d_skip.astype(f32).reshape(SSM_GROUPS, SSM_GROUP) * uf)
    return y.reshape(B, L, SSM_WIDTH).astype(u.dtype)


def causal_dwconv(x, w, b):
    y = lax.conv_general_dilated(
        x, w[:, None, :].astype(x.dtype), window_strides=(1,), padding=[(CONV_WIDTH - 1, 0)],
        dimension_numbers=('NWC', 'WIO', 'NWC'), feature_group_count=x.shape[-1])
    return y + b


def setup_inputs(seed: int = 0) -> dict:
    key = jax.random.key(seed)
    ks = jax.random.split(key, 32)
    f32 = jnp.float32

    def nrm(k, shape, s):
        return jax.random.normal(k, shape, f32) * s

    Ld = DEPTH
    P, G, C = SSM_STATE, SSM_GROUPS, SSM_GROUP
    return {
        "x": nrm(ks[0], (BATCH, SEQ, D_MODEL), 1.0),
        "c": nrm(ks[1], (BATCH, D_MODEL), 1.0),
        "w_ada": nrm(ks[2], (Ld, D_MODEL, 6 * D_MODEL), D_MODEL ** -0.5),
        "b_ada": nrm(ks[3], (Ld, 6 * D_MODEL), 0.02),
        "norm1_g": 1.0 + nrm(ks[4], (Ld, D_MODEL), 0.02),
        "norm2_g": 1.0 + nrm(ks[5], (Ld, D_MODEL), 0.02),
        "w_in": nrm(ks[6], (Ld, D_MODEL, PROJ_IN), D_MODEL ** -0.5),
        "q_norm_g": 1.0 + nrm(ks[7], (Ld, ATTN_HEAD_DIM), 0.02),
        "k_norm_g": 1.0 + nrm(ks[8], (Ld, ATTN_HEAD_DIM), 0.02),
        "lambda_q1": nrm(ks[9], (Ld, ATTN_HEAD_DIM), 0.1),
        "lambda_k1": nrm(ks[10], (Ld, ATTN_HEAD_DIM), 0.1),
        "lambda_q2": nrm(ks[11], (Ld, ATTN_HEAD_DIM), 0.1),
        "lambda_k2": nrm(ks[12], (Ld, ATTN_HEAD_DIM), 0.1),
        "attn_sub_norm_g": 1.0 + nrm(ks[13], (Ld, VALUE_DIM), 0.02),
        "ssm_a_re": -0.5 + nrm(ks[14], (Ld, G, P), 0.01),
        "ssm_a_im": math.pi * jnp.arange(P, dtype=f32) + nrm(ks[15], (Ld, G, P), 0.01),
        "ssm_log_dt": jax.random.uniform(ks[16], (Ld, G), f32, math.log(DT_MIN), math.log(DT_MAX)),
        "ssm_b_re": nrm(ks[17], (Ld, G, P, C), (2 * C) ** -0.5),
        "ssm_b_im": nrm(ks[18], (Ld, G, P, C), (2 * C) ** -0.5),
        "ssm_c_re": nrm(ks[19], (Ld, G, C, P), P ** -0.5),
        "ssm_c_im": nrm(ks[20], (Ld, G, C, P), P ** -0.5),
        "ssm_d": nrm(ks[21], (Ld, SSM_WIDTH), 1.0),
        "w_glu": nrm(ks[22], (Ld, SSM_WIDTH, SSM_WIDTH), SSM_WIDTH ** -0.5),
        "b_glu": nrm(ks[23], (Ld, SSM_WIDTH), 0.02),
        "w_out": nrm(ks[24], (Ld, MIX_WIDTH, D_MODEL), MIX_WIDTH ** -0.5),
        "w_up": nrm(ks[25], (Ld, D_MODEL, 2 * D_FF), D_MODEL ** -0.5),
        "conv_w": nrm(ks[26], (Ld, CONV_WIDTH, D_FF), CONV_WIDTH ** -0.5),
        "conv_b": nrm(ks[27], (Ld, D_FF), 0.02),
        "w_down": nrm(ks[28], (Ld, D_FF, D_MODEL), D_FF ** -0.5),
    }


def reference(x, c, w_ada, b_ada, norm1_g, norm2_g, w_in, q_norm_g, k_norm_g,
              lambda_q1, lambda_k1, lambda_q2, lambda_k2, attn_sub_norm_g,
              ssm_a_re, ssm_a_im, ssm_log_dt, ssm_b_re, ssm_b_im, ssm_c_re, ssm_c_im, ssm_d,
              w_glu, b_glu, w_out, w_up, conv_w, conv_b, w_down):
    B, L, _ = x.shape
    H, d = ATTN_HEADS, ATTN_HEAD_DIM
    c_act = jax.nn.silu(c)
    for l in range(DEPTH):
        lambda_init = 0.8 - 0.6 * math.exp(-0.3 * l)
        mod = c_act @ w_ada[l] + b_ada[l]
        shift1, scale1, gate1, shift2, scale2, gate2 = jnp.split(mod, 6, axis=-1)

        h = modulate(rms_norm(x, norm1_g[l]), shift1, scale1)
        proj = h @ w_in[l]
        q, k, v, u = jnp.split(proj, [ATTN_WIDTH, 2 * ATTN_WIDTH, 3 * ATTN_WIDTH], axis=-1)

        q = rms_norm(q.reshape(B, L, H, 2, d), q_norm_g[l]).transpose(3, 0, 2, 1, 4)
        k = rms_norm(k.reshape(B, L, H, 2, d), k_norm_g[l]).transpose(3, 0, 2, 1, 4)
        v = v.reshape(B, L, H, VALUE_DIM).transpose(0, 2, 1, 3)
        lam = (jnp.exp(jnp.sum(lambda_q1[l].astype(jnp.float32) * lambda_k1[l].astype(jnp.float32)))
               - jnp.exp(jnp.sum(lambda_q2[l].astype(jnp.float32) * lambda_k2[l].astype(jnp.float32)))
               + lambda_init)
        attn = diff_attention(q[0], q[1], k[0], k[1], v, lam)
        attn = (rms_norm(attn, attn_sub_norm_g[l]) * (1.0 - lambda_init)).reshape(B, L, ATTN_WIDTH)

        y = s5_ssm(u, ssm_a_re[l], ssm_a_im[l], ssm_log_dt[l], ssm_b_re[l], ssm_b_im[l],
                   ssm_c_re[l], ssm_c_im[l], ssm_d[l])
        y = jax.nn.gelu(y)
        y = y * jax.nn.sigmoid(y @ w_glu[l] + b_glu[l])

        mix = jnp.concatenate([attn, y], axis=-1) @ w_out[l]
        x = x + gate1[:, None, :] * mix

        h = modulate(rms_norm(x, norm2_g[l]), shift2, scale2)
        a, g = jnp.split(h @ w_up[l], 2, axis=-1)
        a = causal_dwconv(a, conv_w[l], conv_b[l])
        ffn = (jax.nn.gelu(a) * g) @ w_down[l]
        x = x + gate2[:, None, :] * ffn
    return x
```

```python
import functools
import math

import jax
import jax.numpy as jnp
from jax import lax
from jax.experimental import pallas as pl
from jax.experimental.pallas import tpu as pltpu

F32 = jnp.float32
BF16 = jnp.bfloat16

EPS = 1e-6
ATTN_HEADS = 8
SSM_GROUP = 16
SSM_CHUNK = 16
LANES = 128
CONV_HALO = 8
VMEM_LIMIT = 56 * 1024 * 1024


def _cparams(*sem):
    return pltpu.CompilerParams(dimension_semantics=sem, vmem_limit_bytes=VMEM_LIMIT)


def _dot(a, b):
    return jnp.dot(a, b, preferred_element_type=F32)


def _dot_nt(a, b):
    return lax.dot_general(a, b, (((1,), (1,)), ((), ())), preferred_element_type=F32)


def _ada_kernel(c_ref, w_ref, b_ref, o_ref):
    c = c_ref[...]
    o_ref[...] = _dot(c * jax.nn.sigmoid(c), w_ref[...]) + b_ref[...]


def _ada(c, w, b, tn=1024):
    bsz, d = c.shape
    n = w.shape[1]
    return pl.pallas_call(
        _ada_kernel,
        out_shape=jax.ShapeDtypeStruct((bsz, n), F32),
        grid=(n // tn,),
        in_specs=[pl.BlockSpec((bsz, d), lambda j: (0, 0)),
                  pl.BlockSpec((d, tn), lambda j: (0, j)),
                  pl.BlockSpec((1, tn), lambda j: (0, j))],
        out_specs=pl.BlockSpec((bsz, tn), lambda j: (0, j)),
        compiler_params=_cparams("arbitrary"),
        name="ada_ln",
    )(c, w, b.reshape(1, n))


def _norm_modulate(x, g, shift, scale):
    y = x * lax.rsqrt(jnp.mean(x * x, axis=-1, keepdims=True) + EPS)
    return y * g * (1.0 + scale) + shift


def _half_rms_norm(a, gain, post_scale):
    outs = []
    half = LANES // 2
    for cidx in range(a.shape[1] // LANES):
        blk = a[:, cidx * LANES:(cidx + 1) * LANES]
        sq = blk * blk
        lo = lax.broadcasted_iota(jnp.int32, blk.shape, 1) < half
        s_lo = jnp.sum(jnp.where(lo, sq, 0.0), axis=-1, keepdims=True)
        s_hi = jnp.sum(jnp.where(lo, 0.0, sq), axis=-1, keepdims=True)
        ms = jnp.where(lo, s_lo, s_hi) * (1.0 / half)
        outs.append(blk * lax.rsqrt(ms + EPS) * (gain * post_scale))
    return jnp.concatenate(outs, axis=-1)


def _inproj_kernel(x_ref, g_ref, shift_ref, scale_ref, w_ref, qg_ref, kg_ref,
                   qkv_ref, u_ref, h_ref, *, q_scale):
    j = pl.program_id(1)

    @pl.when(j == 0)
    def _():
        h_ref[...] = _norm_modulate(x_ref[...], g_ref[...], shift_ref[...], scale_ref[...]).astype(BF16)

    acc = _dot(h_ref[...], w_ref[...])

    @pl.when(j == 0)
    def _():
        qkv_ref[...] = _half_rms_norm(acc, qg_ref[...], q_scale).astype(BF16)

    @pl.when(j == 1)
    def _():
        qkv_ref[...] = _half_rms_norm(acc, kg_ref[...], 1.0).astype(BF16)

    @pl.when(j == 2)
    def _():
        qkv_ref[...] = acc.astype(BF16)

    @pl.when(j == 3)
    def _():
        u_ref[...] = acc


def _inproj(x2, mod4, norm_g, w_bf, qg, kg, seq, tm):
    t, d = x2.shape
    aw = w_bf.shape[1] // 4
    tiles_per_seq = seq // tm
    bidx = lambda i: i // tiles_per_seq
    qg2 = jnp.tile(qg, 2).reshape(1, LANES)
    kg2 = jnp.tile(kg, 2).reshape(1, LANES)
    head_dim = qg.shape[0]
    return pl.pallas_call(
        functools.partial(_inproj_kernel, q_scale=head_dim ** -0.5),
        out_shape=(jax.ShapeDtypeStruct((t, 3 * aw), BF16), jax.ShapeDtypeStruct((t, aw), F32)),
        grid=(t // tm, 4),
        in_specs=[pl.BlockSpec((tm, d), lambda i, j: (i, 0)),
                  pl.BlockSpec((1, d), lambda i, j: (0, 0)),
                  pl.BlockSpec((None, None, 1, d), lambda i, j: (bidx(i), 0, 0, 0)),
                  pl.BlockSpec((None, None, 1, d), lambda i, j: (bidx(i), 1, 0, 0)),
                  pl.BlockSpec((d, aw), lambda i, j: (0, j)),
                  pl.BlockSpec((1, LANES), lambda i, j: (0, 0)),
                  pl.BlockSpec((1, LANES), lambda i, j: (0, 0))],
        out_specs=(pl.BlockSpec((tm, aw), lambda i, j: (i, jnp.minimum(j, 2))),
                   pl.BlockSpec((tm, aw), lambda i, j: (i, 0))),
        scratch_shapes=[pltpu.VMEM((tm, d), BF16)],
        compiler_params=_cparams("arbitrary", "arbitrary"),
        name="in_proj",
    )(x2, norm_g.reshape(1, d), mod4, mod4, w_bf, qg2, kg2)


def _attn_kernel(slopes_ref, lamv_ref, gsub_ref, q_ref, k_ref, v_ref, o_ref,
                 m_ref, l_ref, acc_ref, *, tq, lambda_init):
    h = pl.program_id(1)
    i = pl.program_id(2)
    slope = slopes_ref[h]
    q = q_ref[...]
    lane = lax.broadcasted_iota(jnp.int32, q.shape, 1)
    zero = jnp.zeros_like(q)
    half = LANES // 2
    q_maps = (jnp.where(lane < half, q, zero), jnp.where(lane >= half, q, zero))

    m_ref[...] = jnp.full(m_ref.shape, -jnp.inf, F32)
    l_ref[...] = jnp.zeros(l_ref.shape, F32)
    acc_ref[...] = jnp.zeros(acc_ref.shape, F32)

    def block(j, diagonal):
        start = pl.multiple_of(j * tq, tq)
        kb = k_ref[pl.ds(start, tq), :]
        vb = v_ref[pl.ds(start, tq), :]
        kpos = (j * tq + lax.broadcasted_iota(jnp.int32, (1, tq), 1)).astype(F32)
        bias = slope * kpos
        for idx in range(2):
            s = _dot_nt(q_maps[idx], kb) + bias
            if diagonal:
                row = lax.broadcasted_iota(jnp.int32, s.shape, 0)
                col = lax.broadcasted_iota(jnp.int32, s.shape, 1)
                s = jnp.where(col <= row, s, -jnp.inf)
            m_old = m_ref[idx]
            m_new = jnp.maximum(m_old, jnp.max(s, axis=-1, keepdims=True))
            alpha = jnp.exp(m_old - m_new)
            p = jnp.exp(s - m_new)
            l_ref[idx] = alpha * l_ref[idx] + jnp.sum(p, axis=-1, keepdims=True)
            acc_ref[idx] = alpha * acc_ref[idx] + _dot(p.astype(BF16), vb)
            m_ref[idx] = m_new

    def body(j, carry):
        block(j, False)
        return carry

    lax.fori_loop(0, i, body, 0)
    block(i, True)

    lv = lamv_ref[...]
    lam = (jnp.exp(jnp.sum(lv[0:1] * lv[1:2], axis=-1, keepdims=True))
           - jnp.exp(jnp.sum(lv[2:3] * lv[3:4], axis=-1, keepdims=True)) + lambda_init)
    o = acc_ref[0] / l_ref[0] - lam * (acc_ref[1] / l_ref[1])
    o = o * lax.rsqrt(jnp.mean(o * o, axis=-1, keepdims=True) + EPS)
    o_ref[...] = (o * gsub_ref[...] * (1.0 - lambda_init)).astype(BF16)


def _attention(qkv, lamv, gsub, bsz, seq, lambda_init, tq):
    t = qkv.shape[0]
    aw = qkv.shape[1] // 3
    heads = aw // LANES
    nq = seq // tq
    slopes = jnp.asarray([2.0 ** (-8.0 * (i + 1) / heads) for i in range(heads)], F32)
    return pl.pallas_call(
        functools.partial(_attn_kernel, tq=tq, lambda_init=lambda_init),
        out_shape=jax.ShapeDtypeStruct((t, aw), BF16),
        grid=(bsz, heads, nq),
        in_specs=[pl.BlockSpec(memory_space=pltpu.SMEM),
                  pl.BlockSpec((4, LANES // 2), lambda b, h, i: (0, 0)),
                  pl.BlockSpec((1, LANES), lambda b, h, i: (0, 0)),
                  pl.BlockSpec((tq, LANES), lambda b, h, i: (b * nq + i, h)),
                  pl.BlockSpec((seq, LANES), lambda b, h, i: (b, heads + h)),
                  pl.BlockSpec((seq, LANES), lambda b, h, i: (b, 2 * heads + h))],
        out_specs=pl.BlockSpec((tq, LANES), lambda b, h, i: (b * nq + i, h)),
        scratch_shapes=[pltpu.VMEM((2, tq, 1), F32), pltpu.VMEM((2, tq, 1), F32),
                        pltpu.VMEM((2, tq, LANES), F32)],
        compiler_params=_cparams("arbitrary", "arbitrary", "arbitrary"),
        name="diff_attn",
    )(slopes, lamv, gsub.reshape(1, LANES), qkv, qkv, qkv)


def _ssm_matrices(a_re, a_im, log_dt, b_re, b_im, c_re, c_im, d_skip):
    g, p = a_re.shape
    cch = b_re.shape[-1]
    q = SSM_CHUNK
    dt = jnp.exp(log_dt.astype(F32))[:, None]
    ar, ai = a_re.astype(F32), a_im.astype(F32)
    mag = jnp.exp(dt * ar)
    abar_re, abar_im = mag * jnp.cos(dt * ai), mag * jnp.sin(dt * ai)
    den = ar * ar + ai * ai
    nr, ni = abar_re - 1.0, abar_im
    coef_re = (nr * ar + ni * ai) / den
    coef_im = (ni * ar - nr * ai) / den
    br, bi = b_re.astype(F32), b_im.astype(F32)
    bbar_re = coef_re[..., None] * br - coef_im[..., None] * bi
    bbar_im = coef_re[..., None] * bi + coef_im[..., None] * br

    def power(n):
        n = jnp.asarray(n, F32)[:, None, None]
        m = jnp.exp(n * (dt * ar)[None])
        return m * jnp.cos(n * (dt * ai)[None]), m * jnp.sin(n * (dt * ai)[None])

    cr, ci = c_re.astype(F32), c_im.astype(F32)
    pr, pi = power(range(q))
    ca_re = cr[None] * pr[:, :, None, :] - ci[None] * pi[:, :, None, :]
    ca_im = cr[None] * pi[:, :, None, :] + ci[None] * pr[:, :, None, :]
    kern = (jnp.einsum('tgcp,gpd->tgcd', ca_re, bbar_re, precision=lax.Precision.HIGHEST)
            - jnp.einsum('tgcp,gpd->tgcd', ca_im, bbar_im, precision=lax.Precision.HIGHEST))
    jj = jnp.arange(q)
    tau = jj[None, :] - jj[:, None]
    tmat = jnp.where((tau >= 0)[:, :, None, None, None],
                     kern[jnp.clip(tau, 0, q - 1)], 0.0)
    tmat = tmat.transpose(2, 0, 4, 1, 3).reshape(g, q * cch, q * cch)

    er, ei = power(q - 1 - jj)
    bp_re = er[..., None] * bbar_re[None] - ei[..., None] * bbar_im[None]
    bp_im = er[..., None] * bbar_im[None] + ei[..., None] * bbar_re[None]
    bp_re = bp_re.transpose(1, 0, 3, 2).reshape(g, q * cch, p)
    bp_im = bp_im.transpose(1, 0, 3, 2).reshape(g, q * cch, p)

    fr, fi = power(jj + 1)
    cp_re = cr[None] * fr[:, :, None, :] - ci[None] * fi[:, :, None, :]
    cp_im = -(cr[None] * fi[:, :, None, :] + ci[None] * fr[:, :, None, :])
    cp_re = cp_re.transpose(1, 3, 0, 2).reshape(g, p, q * cch)
    cp_im = cp_im.transpose(1, 3, 0, 2).reshape(g, p, q * cch)

    aq_re, aq_im = power([q])
    aq_re, aq_im = aq_re[0], aq_im[0]

    np_ = g // 2
    w = q * cch
    zb = jnp.zeros((np_, w, p), F32)
    bp_re2, bp_im2 = bp_re.reshape(np_, 2, w, p), bp_im.reshape(np_, 2, w, p)
    bd = jnp.concatenate([
        jnp.concatenate([bp_re2[:, 0], zb, bp_im2[:, 0], zb], axis=-1),
        jnp.concatenate([zb, bp_re2[:, 1], zb, bp_im2[:, 1]], axis=-1)], axis=1)
    zc = jnp.zeros((np_, p, w), F32)
    cp_re2, cp_im2 = cp_re.reshape(np_, 2, p, w), cp_im.reshape(np_, 2, p, w)
    cd = jnp.concatenate([
        jnp.concatenate([cp_re2[:, 0], zc], axis=-1),
        jnp.concatenate([zc, cp_re2[:, 1]], axis=-1),
        jnp.concatenate([cp_im2[:, 0], zc], axis=-1),
        jnp.concatenate([zc, cp_im2[:, 1]], axis=-1)], axis=1)
    aq_re2 = aq_re.reshape(np_, 1, 2 * p)
    aq_im2 = aq_im.reshape(np_, 1, 2 * p)
    dvec = jnp.tile(d_skip.astype(F32).reshape(np_, 2, 1, cch), (1, 1, q, 1)).reshape(np_, 1, 2 * w)
    return tmat.astype(BF16), bd.astype(BF16), cd.astype(BF16), aq_re2, aq_im2, dvec


def _ssm_kernel(u_ref, t_ref, bd_ref, cd_ref, are_ref, aim_ref, d_ref, y_ref,
                sre_ref, sim_ref, xre_ref, xim_ref, *, npair, bsz, nchunk):
    sl = 2 * (LANES // 2)
    w = u_ref.shape[-1] // 2
    for p in range(npair):
        u = u_ref[p]
        ub = u.astype(BF16)
        s = _dot(ub, bd_ref[p])
        y_intra = jnp.concatenate([_dot(ub[:, :w], t_ref[2 * p]),
                                   _dot(ub[:, w:], t_ref[2 * p + 1])], axis=-1)
        y_ref[p] = y_intra + d_ref[p] * u
        for b in range(bsz):
            lo = (b * npair + p) * sl
            sre_ref[:, lo:lo + sl] = s[b * nchunk:(b + 1) * nchunk, :sl]
            sim_ref[:, lo:lo + sl] = s[b * nchunk:(b + 1) * nchunk, sl:]

    are = jnp.concatenate([are_ref[p] for _ in range(bsz) for p in range(npair)], axis=-1)
    aim = jnp.concatenate([aim_ref[p] for _ in range(bsz) for p in range(npair)], axis=-1)

    def step(k, carry):
        xr, xi = carry
        xre_ref[pl.ds(k, 1), :] = xr
        xim_ref[pl.ds(k, 1), :] = xi
        nr = are * xr - aim * xi + sre_ref[pl.ds(k, 1), :]
        ni = are * xi + aim * xr + sim_ref[pl.ds(k, 1), :]
        return nr, ni

    zero = jnp.zeros((1, sre_ref.shape[1]), F32)
    lax.fori_loop(0, nchunk, step, (zero, zero), unroll=4)

    for p in range(npair):
        xs = []
        for b in range(bsz):
            lo = (b * npair + p) * sl
            xs.append(jnp.concatenate([xre_ref[:, lo:lo + sl], xim_ref[:, lo:lo + sl]], axis=-1))
        x = jnp.concatenate(xs, axis=0).astype(BF16)
        y_ref[p] = y_ref[p] + _dot(x, cd_ref[p])


def _ssm(u2, mats, bsz, seq, npair=4):
    tmat, bd, cd, aq_re, aq_im, dvec = mats
    t, sw = u2.shape
    q = SSM_CHUNK
    cch = SSM_GROUP
    pairs = sw // (2 * cch)
    nchunk = seq // q
    rows = bsz * nchunk
    w2 = 2 * q * cch
    uc = (u2.reshape(bsz, nchunk, q, pairs, 2, cch).transpose(3, 0, 1, 4, 2, 5)
          .reshape(pairs, rows, w2))
    st = bd.shape[-1]
    yc = pl.pallas_call(
        functools.partial(_ssm_kernel, npair=npair, bsz=bsz, nchunk=nchunk),
        out_shape=jax.ShapeDtypeStruct((pairs, rows, w2), F32),
        grid=(pairs // npair,),
        in_specs=[pl.BlockSpec((npair, rows, w2), lambda i: (i, 0, 0)),
                  pl.BlockSpec((2 * npair, w2 // 2, w2 // 2), lambda i: (i, 0, 0)),
                  pl.BlockSpec((npair, w2, st), lambda i: (i, 0, 0)),
                  pl.BlockSpec((npair, st, w2), lambda i: (i, 0, 0)),
                  pl.BlockSpec((npair, 1, st // 2), lambda i: (i, 0, 0)),
                  pl.BlockSpec((npair, 1, st // 2), lambda i: (i, 0, 0)),
                  pl.BlockSpec((npair, 1, w2), lambda i: (i, 0, 0))],
        out_specs=pl.BlockSpec((npair, rows, w2), lambda i: (i, 0, 0)),
        scratch_shapes=[pltpu.VMEM((nchunk, bsz * npair * st // 2), F32) for _ in range(4)],
        compiler_params=_cparams("arbitrary"),
        name="s5_scan",
    )(uc, tmat, bd, cd, aq_re, aq_im, dvec)
    return (yc.reshape(pairs, bsz, nchunk, 2, q, cch).transpose(1, 2, 4, 0, 3, 5)
            .reshape(t, sw))


def _mix_kernel(x_ref, attn_ref, y_ref, wglu_ref, bglu_ref, wout_ref, gate_ref,
                g_ref, shift_ref, scale_ref, x1_ref, h2_ref):
    aw = attn_ref.shape[1]
    yg = jax.nn.gelu(y_ref[...])
    z = _dot(yg.astype(BF16), wglu_ref[...]) + bglu_ref[...]
    yy = yg * jax.nn.sigmoid(z)
    mix = _dot(attn_ref[...], wout_ref[:aw, :]) + _dot(yy.astype(BF16), wout_ref[aw:, :])
    x1 = x_ref[...] + gate_ref[...] * mix
    x1_ref[...] = x1
    h2_ref[...] = _norm_modulate(x1, g_ref[...], shift_ref[...], scale_ref[...]).astype(BF16)


def _mix(x2, attn, y, wglu_bf, bglu, wout_bf, mod4, norm_g, seq, tm):
    t, d = x2.shape
    aw = attn.shape[1]
    tiles_per_seq = seq // tm
    bidx = lambda i: i // tiles_per_seq
    modspec = lambda k: pl.BlockSpec((None, None, 1, d), lambda i: (bidx(i), k, 0, 0))
    return pl.pallas_call(
        _mix_kernel,
        out_shape=(jax.ShapeDtypeStruct((t, d), F32), jax.ShapeDtypeStruct((t, d), BF16)),
        grid=(t // tm,),
        in_specs=[pl.BlockSpec((tm, d), lambda i: (i, 0)),
                  pl.BlockSpec((tm, aw), lambda i: (i, 0)),
                  pl.BlockSpec((tm, aw), lambda i: (i, 0)),
                  pl.BlockSpec(wglu_bf.shape, lambda i: (0, 0)),
                  pl.BlockSpec((1, aw), lambda i: (0, 0)),
                  pl.BlockSpec(wout_bf.shape, lambda i: (0, 0)),
                  modspec(2),
                  pl.BlockSpec((1, d), lambda i: (0, 0)),
                  modspec(3), modspec(4)],
        out_specs=(pl.BlockSpec((tm, d), lambda i: (i, 0)), pl.BlockSpec((tm, d), lambda i: (i, 0))),
        compiler_params=_cparams("arbitrary"),
        name="mix_out_proj",
    )(x2, attn, y, wglu_bf, bglu.reshape(1, aw), wout_bf, mod4, norm_g.reshape(1, d), mod4, mod4)


def _ffn_kernel(h_ref, halo_ref, wa_ref, wg_ref, cw_ref, cb_ref, wd_ref, x1_ref, gate_ref,
                o_ref, acc_ref, *, tiles_per_seq):
    i = pl.program_id(0)
    f = pl.program_id(1)
    h = h_ref[...]
    a = _dot(h, wa_ref[...])
    g = _dot(h, wg_ref[...])
    ah = _dot(halo_ref[...], wa_ref[...])
    ah = jnp.where(i % tiles_per_seq == 0, 0.0, ah)
    row = lax.broadcasted_iota(jnp.int32, a.shape, 0)
    a1 = jnp.where(row == 0, ah[CONV_HALO - 1:CONV_HALO], pltpu.roll(a, 1, axis=0))
    a2 = pltpu.roll(a, 2, axis=0)
    a2 = jnp.where(row == 0, ah[CONV_HALO - 2:CONV_HALO - 1], a2)
    a2 = jnp.where(row == 1, ah[CONV_HALO - 1:CONV_HALO], a2)
    cw = cw_ref[...]
    conv = cw[2:3] * a + cw[1:2] * a1 + cw[0:1] * a2 + cb_ref[...]
    act = (jax.nn.gelu(conv) * g).astype(BF16)
    part = _dot(act, wd_ref[...])

    @pl.when(f == 0)
    def _():
        acc_ref[...] = part

    @pl.when(f > 0)
    def _():
        acc_ref[...] += part

    @pl.when(f == pl.num_programs(1) - 1)
    def _():
        o_ref[...] = x1_ref[...] + gate_ref[...] * acc_ref[...]


def _ffn(h2, x1, wup_bf, conv_w, conv_b, wdown_bf, mod4, seq, tm, tf):
    t, d = h2.shape
    dff = wdown_bf.shape[0]
    nf = dff // tf
    tiles_per_seq = seq // tm
    bidx = lambda i: i // tiles_per_seq
    halo_blocks = tm // CONV_HALO
    return pl.pallas_call(
        functools.partial(_ffn_kernel, tiles_per_seq=tiles_per_seq),
        out_shape=jax.ShapeDtypeStruct((t, d), F32),
        grid=(t // tm, nf),
        in_specs=[pl.BlockSpec((tm, d), lambda i, f: (i, 0)),
                  pl.BlockSpec((CONV_HALO, d), lambda i, f: (jnp.maximum(i * halo_blocks - 1, 0), 0)),
                  pl.BlockSpec((d, tf), lambda i, f: (0, f)),
                  pl.BlockSpec((d, tf), lambda i, f: (0, nf + f)),
                  pl.BlockSpec((conv_w.shape[0], tf), lambda i, f: (0, f)),
                  pl.BlockSpec((1, tf), lambda i, f: (0, f)),
                  pl.BlockSpec((tf, d), lambda i, f: (f, 0)),
                  pl.BlockSpec((tm, d), lambda i, f: (i, 0)),
                  pl.BlockSpec((None, None, 1, d), lambda i, f: (bidx(i), 5, 0, 0))],
        out_specs=pl.BlockSpec((tm, d), lambda i, f: (i, 0)),
        scratch_shapes=[pltpu.VMEM((tm, d), F32)],
        compiler_params=_cparams("arbitrary", "arbitrary"),
        name="conv_ffn",
    )(h2, h2, wup_bf, wup_bf, conv_w, conv_b.reshape(1, dff), wdown_bf, x1, mod4)


def kernel(x, c, w_ada, b_ada, norm1_g, norm2_g, w_in, q_norm_g, k_norm_g, lambda_q1, lambda_k1,
           lambda_q2, lambda_k2, attn_sub_norm_g, ssm_a_re, ssm_a_im, ssm_log_dt, ssm_b_re, ssm_b_im,
           ssm_c_re, ssm_c_im, ssm_d, w_glu, b_glu, w_out, w_up, conv_w, conv_b, w_down):
    bsz, seq, d = x.shape
    depth = w_ada.shape[0]
    tm = min(512, seq)
    x2 = x.reshape(bsz * seq, d)
    for l in range(depth):
        lambda_init = 0.8 - 0.6 * math.exp(-0.3 * l)
        mod = _ada(c, w_ada[l], b_ada[l])
        mod4 = mod.reshape(bsz, 6, 1, d)
        qkv, u = _inproj(x2, mod4, norm1_g[l], w_in[l].astype(BF16), q_norm_g[l], k_norm_g[l], seq, tm)
        lamv = jnp.stack([lambda_q1[l], lambda_k1[l], lambda_q2[l], lambda_k2[l]]).astype(F32)
        attn = _attention(qkv, lamv, attn_sub_norm_g[l], bsz, seq, lambda_init, tm)
        mats = _ssm_matrices(ssm_a_re[l], ssm_a_im[l], ssm_log_dt[l], ssm_b_re[l], ssm_b_im[l],
                             ssm_c_re[l], ssm_c_im[l], ssm_d[l])
        y = _ssm(u, mats, bsz, seq)
        x1, h2 = _mix(x2, attn, y, w_glu[l].astype(BF16), b_glu[l], w_out[l].astype(BF16), mod4,
                      norm2_g[l], seq, min(256, seq))
        x2 = _ffn(h2, x1, w_up[l].astype(BF16), conv_w[l], conv_b[l], w_down[l].astype(BF16), mod4,
                  seq, tm, 512)
    return x2.reshape(bsz, seq, d)
```

```python
import functools
import math

import jax
import jax.numpy as jnp
from jax import lax
from jax.experimental import pallas as pl
from jax.experimental.pallas import tpu as pltpu

F32 = jnp.float32
BF16 = jnp.bfloat16

EPS = 1e-6
LANES = 128
SUBLANES = 8
MXU_WIDTH = 256
SSM_GROUP = 16
SSM_CHUNK = 8
CONV_HALO = 2 * SUBLANES
VMEM_LIMIT = 56 * 1024 * 1024
ROW_TILE = 512
FFN_TF = 512


def _cparams(*sem):
    return pltpu.CompilerParams(dimension_semantics=sem, vmem_limit_bytes=VMEM_LIMIT)


def _dot(a, b):
    return jnp.dot(a, b, preferred_element_type=F32)


def _dot_nt(a, b):
    return lax.dot_general(a, b, (((1,), (1,)), ((), ())), preferred_element_type=F32)


def _resident(shape):
    return pl.BlockSpec(shape, lambda *_: (0,) * len(shape), pipeline_mode=pl.Buffered(1))


def _ada_kernel(c_ref, w_ref, b_ref, o_ref):
    c = c_ref[...]
    o_ref[...] = _dot(c * jax.nn.sigmoid(c), w_ref[...]) + b_ref[...]


def _ada(c, w, b, tn=1024):
    bsz, d = c.shape
    n = w.shape[1]
    return pl.pallas_call(
        _ada_kernel,
        out_shape=jax.ShapeDtypeStruct((bsz, n), F32),
        grid=(n // tn,),
        in_specs=[pl.BlockSpec((bsz, d), lambda j: (0, 0)),
                  pl.BlockSpec((d, tn), lambda j: (0, j)),
                  pl.BlockSpec((1, tn), lambda j: (0, j))],
        out_specs=pl.BlockSpec((bsz, tn), lambda j: (0, j)),
        compiler_params=_cparams("arbitrary"),
        name="ada_ln",
    )(c, w, b.reshape(1, n))


def _norm_modulate(x, g, shift, scale):
    y = x * lax.rsqrt(jnp.mean(x * x, axis=-1, keepdims=True) + EPS)
    return y * g * (1.0 + scale) + shift


def _half_rms_norm(a, gain, post_scale):
    outs = []
    half = LANES // 2
    for cidx in range(a.shape[1] // LANES):
        blk = a[:, cidx * LANES:(cidx + 1) * LANES]
        sq = blk * blk
        lo = lax.broadcasted_iota(jnp.int32, blk.shape, 1) < half
        s_lo = jnp.sum(jnp.where(lo, sq, 0.0), axis=-1, keepdims=True)
        s_hi = jnp.sum(jnp.where(lo, 0.0, sq), axis=-1, keepdims=True)
        ms = jnp.where(lo, s_lo, s_hi) * (1.0 / half)
        outs.append(blk * lax.rsqrt(ms + EPS) * (gain * post_scale))
    return jnp.concatenate(outs, axis=-1)


def _inproj_kernel(x_ref, g_ref, shift_ref, scale_ref, w_ref, qg_ref, kg_ref,
                   qkv_ref, u_ref, *, q_scale):
    aw = u_ref.shape[1]
    h = _norm_modulate(x_ref[...], g_ref[...], shift_ref[...], scale_ref[...]).astype(BF16)
    qkv_ref[:, :aw] = _half_rms_norm(_dot(h, w_ref[:, :aw]), qg_ref[...], q_scale).astype(BF16)
    qkv_ref[:, aw:2 * aw] = _half_rms_norm(_dot(h, w_ref[:, aw:2 * aw]), kg_ref[...], 1.0).astype(BF16)
    qkv_ref[:, 2 * aw:] = _dot(h, w_ref[:, 2 * aw:3 * aw]).astype(BF16)
    u_ref[...] = _dot(h, w_ref[:, 3 * aw:])


def _inproj(x2, mod4, norm_g, w_bf, qg, kg, seq, tm):
    t, d = x2.shape
    aw = w_bf.shape[1] // 4
    tiles_per_seq = seq // tm
    bidx = lambda i: i // tiles_per_seq
    qg2 = jnp.tile(qg, 2).reshape(1, LANES)
    kg2 = jnp.tile(kg, 2).reshape(1, LANES)
    head_dim = qg.shape[0]
    return pl.pallas_call(
        functools.partial(_inproj_kernel, q_scale=head_dim ** -0.5 * math.log2(math.e)),
        out_shape=(jax.ShapeDtypeStruct((t, 3 * aw), BF16), jax.ShapeDtypeStruct((t, aw), F32)),
        grid=(t // tm,),
        in_specs=[pl.BlockSpec((tm, d), lambda i: (i, 0)),
                  pl.BlockSpec((1, d), lambda i: (0, 0)),
                  pl.BlockSpec((None, None, 1, d), lambda i: (bidx(i), 0, 0, 0)),
                  pl.BlockSpec((None, None, 1, d), lambda i: (bidx(i), 1, 0, 0)),
                  _resident(w_bf.shape),
                  pl.BlockSpec((1, LANES), lambda i: (0, 0)),
                  pl.BlockSpec((1, LANES), lambda i: (0, 0))],
        out_specs=(pl.BlockSpec((tm, 3 * aw), lambda i: (i, 0)),
                   pl.BlockSpec((tm, aw), lambda i: (i, 0))),
        compiler_params=_cparams("arbitrary"),
        name="in_proj",
    )(x2, norm_g.reshape(1, d), mod4, mod4, w_bf, qg2, kg2)


ATTN_ROWS = 64
NEG_BIG = -1e30
POS_SPLIT = 64


def _attn_kernel(lamv_ref, gsub_ref, coef_ref, pos_ref, q_ref, k_ref, v_ref, o_ref,
                 kaug_ref, vaug_ref, qaug_ref, s_ref, p_ref, m_ref, alpha_ref, acc_ref,
                 *, tq, lambda_init):
    i = pl.program_id(2)
    half = LANES // 2

    @pl.when(i == 0)
    def _():
        kaug_ref[:, :LANES] = k_ref[...]
        kaug_ref[:, LANES:] = pos_ref[...]
        vaug_ref[:, :LANES] = v_ref[...]
        vaug_ref[:, LANES:] = jnp.ones((v_ref.shape[0], LANES), BF16)

    q = q_ref[...]
    lane = lax.broadcasted_iota(jnp.int32, q.shape, 1)
    zero = jnp.zeros_like(q)
    qaug_ref[0, :, :LANES] = jnp.where(lane < half, q, zero)
    qaug_ref[1, :, :LANES] = jnp.where(lane >= half, q, zero)
    coef = jnp.broadcast_to(coef_ref[...], q.shape)
    qaug_ref[0, :, LANES:] = coef
    qaug_ref[1, :, LANES:] = coef

    m_ref[...] = jnp.full(m_ref.shape, NEG_BIG, F32)
    acc_ref[...] = jnp.zeros(acc_ref.shape, F32)
    alpha_ref[1] = jnp.ones(alpha_ref.shape[1:], F32)
    p_ref[1] = jnp.zeros(p_ref.shape[1:], BF16)

    nsub = tq // ATTN_ROWS

    def scores(j, idx):
        start = pl.multiple_of(j * tq, tq)
        s_ref[idx] = _dot_nt(qaug_ref[idx], kaug_ref[pl.ds(start, tq), :])

    def softmax(idx, diagonal):
        for r in range(nsub):
            rows = slice(r * ATTN_ROWS, (r + 1) * ATTN_ROWS)
            width = min(tq, -(-((r + 1) * ATTN_ROWS) // LANES) * LANES) if diagonal else tq
            s = s_ref[idx, rows, :width]
            if diagonal:
                last = s[:, width - LANES:]
                row = lax.broadcasted_iota(jnp.int32, last.shape, 0) + r * ATTN_ROWS
                col = lax.broadcasted_iota(jnp.int32, last.shape, 1) + (width - LANES)
                last = jnp.where(col <= row, last, NEG_BIG)
                s = last if width == LANES else jnp.concatenate([s[:, :width - LANES], last], axis=-1)
            m_old = m_ref[idx, rows, :]
            m_new = jnp.maximum(m_old, jnp.max(s, axis=-1, keepdims=True))
            alpha_ref[idx, rows, :] = jnp.exp2(m_old - m_new)
            m_ref[idx, rows, :] = m_new
            p_ref[idx, rows, :width] = jnp.exp2(s - jnp.tile(m_new, (1, width // LANES))).astype(BF16)
            if width < tq:
                p_ref[idx, rows, width:] = jnp.zeros((ATTN_ROWS, tq - width), BF16)

    def values(j, idx):
        start = pl.multiple_of(jnp.maximum(j, 0) * tq, tq)
        pv = _dot(p_ref[idx], vaug_ref[pl.ds(start, tq), :])
        acc_ref[idx] = jnp.tile(alpha_ref[idx], (1, 2)) * acc_ref[idx] + pv

    scores(0, 0)

    def body(j, carry):
        scores(j, 1)
        softmax(0, False)
        values(j - 1, 1)
        scores(j + 1, 0)
        softmax(1, False)
        values(j, 0)
        return carry

    lax.fori_loop(0, i, body, 0)
    scores(i, 1)
    softmax(0, True)
    values(i - 1, 1)
    softmax(1, True)
    values(i, 0)
    values(i, 1)

    lv = lamv_ref[...]
    lam = (jnp.exp(jnp.sum(lv[0:1] * lv[1:2], axis=-1, keepdims=True))
           - jnp.exp(jnp.sum(lv[2:3] * lv[3:4], axis=-1, keepdims=True)) + lambda_init)
    a0 = acc_ref[0]
    a1 = acc_ref[1]
    o = a0[:, :LANES] / a0[:, LANES:] - lam * (a1[:, :LANES] / a1[:, LANES:])
    o = o * lax.rsqrt(jnp.mean(o * o, axis=-1, keepdims=True) + EPS)
    o_ref[...] = (o * gsub_ref[...] * (1.0 - lambda_init)).astype(BF16)


def _split_bf16(x, n):
    parts = []
    for _ in range(n):
        p = x.astype(BF16)
        parts.append(p)
        x = x - p.astype(F32)
    return parts


def _attention(qkv, lamv, gsub, bsz, seq, lambda_init, tq):
    t = qkv.shape[0]
    aw = qkv.shape[1] // 3
    heads = aw // LANES
    nq = seq // tq
    c = jnp.asarray([2.0 ** (-8.0 * (i + 1) / heads) * math.log2(math.e) for i in range(heads)], F32)
    cparts = _split_bf16(c, 3)
    coef = jnp.stack([p * POS_SPLIT for p in cparts] + cparts, axis=-1)
    coef = jnp.pad(coef, ((0, 0), (0, LANES - coef.shape[1]))).reshape(heads, 1, LANES)
    kpos = jnp.arange(seq, dtype=jnp.int32)
    hi = (kpos // POS_SPLIT).astype(BF16)
    lo = (kpos % POS_SPLIT).astype(BF16)
    pos = jnp.pad(jnp.stack([hi, hi, hi, lo, lo, lo], axis=-1), ((0, 0), (0, LANES - 6)))
    return pl.pallas_call(
        functools.partial(_attn_kernel, tq=tq, lambda_init=lambda_init),
        out_shape=jax.ShapeDtypeStruct((t, aw), BF16),
        grid=(bsz, heads, nq),
        in_specs=[pl.BlockSpec((4, LANES // 2), lambda b, h, i: (0, 0)),
                  pl.BlockSpec((1, LANES), lambda b, h, i: (0, 0)),
                  pl.BlockSpec((None, 1, LANES), lambda b, h, i: (h, 0, 0)),
                  pl.BlockSpec((seq, LANES), lambda b, h, i: (0, 0)),
                  pl.BlockSpec((tq, LANES), lambda b, h, i: (b * nq + i, h)),
                  pl.BlockSpec((seq, LANES), lambda b, h, i: (b, heads + h)),
                  pl.BlockSpec((seq, LANES), lambda b, h, i: (b, 2 * heads + h))],
        out_specs=pl.BlockSpec((tq, LANES), lambda b, h, i: (b * nq + i, h)),
        scratch_shapes=[pltpu.VMEM((seq, 2 * LANES), BF16), pltpu.VMEM((seq, 2 * LANES), BF16),
                        pltpu.VMEM((2, tq, 2 * LANES), BF16),
                        pltpu.VMEM((2, tq, tq), F32), pltpu.VMEM((2, tq, tq), BF16),
                        pltpu.VMEM((2, tq, LANES), F32), pltpu.VMEM((2, tq, LANES), F32),
                        pltpu.VMEM((2, tq, 2 * LANES), F32)],
        compiler_params=_cparams("arbitrary", "arbitrary", "arbitrary"),
        name="diff_attn",
    )(lamv, gsub.reshape(1, LANES), coef, pos, qkv, qkv, qkv)


def _ssm_matrices(a_re, a_im, log_dt, b_re, b_im, c_re, c_im, nk):
    g, p = a_re.shape
    cch = b_re.shape[-1]
    q = SSM_CHUNK
    gb = LANES // cch
    nb = g // gb
    dt = jnp.exp(log_dt.astype(F32))[:, None]
    ar, ai = a_re.astype(F32), a_im.astype(F32)
    mag = jnp.exp(dt * ar)
    abar_re, abar_im = mag * jnp.cos(dt * ai), mag * jnp.sin(dt * ai)
    den = ar * ar + ai * ai
    nr, ni = abar_re - 1.0, abar_im
    coef_re = (nr * ar + ni * ai) / den
    coef_im = (ni * ar - nr * ai) / den
    br, bi = b_re.astype(F32), b_im.astype(F32)
    bbar_re = coef_re[..., None] * br - coef_im[..., None] * bi
    bbar_im = coef_re[..., None] * bi + coef_im[..., None] * br

    def power(n):
        n = jnp.asarray(n, F32)[:, None, None]
        m = jnp.exp(n * (dt * ar)[None])
        return m * jnp.cos(n * (dt * ai)[None]), m * jnp.sin(n * (dt * ai)[None])

    cr, ci = c_re.astype(F32), c_im.astype(F32)
    eye = jnp.eye(gb, dtype=F32)[None, None, :, None, None, :, None]
    jj = jnp.arange(q)

    pr, pi = power(range(q))
    ca_re = cr[None] * pr[:, :, None, :] - ci[None] * pi[:, :, None, :]
    ca_im = cr[None] * pi[:, :, None, :] + ci[None] * pr[:, :, None, :]
    kern = (jnp.einsum('tgcp,gpd->tgcd', ca_re, bbar_re, precision=lax.Precision.HIGHEST)
            - jnp.einsum('tgcp,gpd->tgcd', ca_im, bbar_im, precision=lax.Precision.HIGHEST))
    tau = jj[None, :] - jj[:, None]
    toep = jnp.where((tau >= 0)[:, :, None, None, None],
                     kern[jnp.clip(tau, 0, q - 1)], 0.0)
    t6 = toep.reshape(q, q, nb, gb, cch, cch).transpose(2, 0, 5, 1, 3, 4)
    tmat = (t6[:, :, None] * eye).reshape(nb, q * LANES, q * LANES)

    er, ei = power(q - 1 - jj)
    bp_re = er[..., None] * bbar_re[None] - ei[..., None] * bbar_im[None]
    bp_im = er[..., None] * bbar_im[None] + ei[..., None] * bbar_re[None]
    bp6 = jnp.stack([x.reshape(q, nb, gb, p, cch).transpose(1, 0, 4, 2, 3) for x in (bp_re, bp_im)],
                    axis=3)
    bd = (bp6[:, :, None] * eye).reshape(nb, q * LANES, 2 * gb * p)

    fr, fi = power(jj + 1)
    cp_re = cr[None] * fr[:, :, None, :] - ci[None] * fi[:, :, None, :]
    cp_im = -(cr[None] * fi[:, :, None, :] + ci[None] * fr[:, :, None, :])
    cp6 = jnp.stack([x.reshape(q, nb, gb, cch, p).transpose(1, 4, 0, 2, 3) for x in (cp_re, cp_im)],
                    axis=1)
    cd = (cp6[:, :, None] * eye).reshape(nb, 2 * gb * p, q * LANES)

    def lanes(x):
        return x.reshape(x.shape[0], nb, gb * p).transpose(1, 0, 2)

    aq = [lanes(x) for x in power([q])]
    aseg = [lanes(x) for x in power([q * nk])]
    return tmat.astype(BF16), bd.astype(BF16), cd.astype(BF16), aq, aseg


def _ssm_kernel(u_ref, tm_ref, bd_ref, cd_ref, aqr_ref, aqi_ref, asr_ref, asi_ref,
                d_ref, y_ref, z_ref, yi_ref, sre_ref, sim_ref, xre_ref, xim_ref, xn_ref, *, bsz, nk):
    q = SSM_CHUNK
    rows = z_ref.shape[0]
    nslab = sre_ref.shape[0]
    half = nslab * LANES
    nseq = rows // nk
    nseg = nseq // bsz
    step_n = MXU_WIDTH

    for j in range(q):
        z_ref[:, j * LANES:(j + 1) * LANES] = u_ref[pl.ds(j, rows, stride=q), :].astype(BF16)
    z = z_ref[...]
    for c in range(0, q * LANES, step_n):
        yi_ref[:, c:c + step_n] = _dot(z, tm_ref[:, c:c + step_n])
    for c in range(0, 2 * half, step_n):
        s = _dot(z, bd_ref[:, c:c + step_n])
        dst = sre_ref if c < half else sim_ref
        for hh in range(step_n // LANES):
            slab = (c % half) // LANES + hh
            for sq in range(nseq):
                dst[slab, pl.ds(sq, nk, stride=nseq), :] = s[sq * nk:(sq + 1) * nk, hh * LANES:(hh + 1) * LANES]

    ar = [jnp.broadcast_to(aqr_ref[:, s * LANES:(s + 1) * LANES], (nseq, LANES)) for s in range(nslab)]
    ai = [jnp.broadcast_to(aqi_ref[:, s * LANES:(s + 1) * LANES], (nseq, LANES)) for s in range(nslab)]

    def step(k, carry):
        xr, xi = carry
        base = pl.multiple_of(k * nseq, nseq)
        nr, ni = [], []
        for s in range(nslab):
            xre_ref[s, pl.ds(base, nseq), :] = xr[s]
            xim_ref[s, pl.ds(base, nseq), :] = xi[s]
            nr.append(ar[s] * xr[s] - ai[s] * xi[s] + sre_ref[s, pl.ds(base, nseq), :])
            ni.append(ar[s] * xi[s] + ai[s] * xr[s] + sim_ref[s, pl.ds(base, nseq), :])
        return tuple(nr), tuple(ni)

    zero = tuple(jnp.zeros((nseq, LANES), F32) for _ in range(nslab))
    er, ei = lax.fori_loop(0, nk, step, (zero, zero), unroll=2)

    cr, ci = [], []
    for s in range(nslab):
        asr = asr_ref[:, s * LANES:(s + 1) * LANES]
        asi = asi_ref[:, s * LANES:(s + 1) * LANES]
        rows_r, rows_i = [], []
        for b in range(bsz):
            cre = jnp.zeros((1, LANES), F32)
            cim = jnp.zeros((1, LANES), F32)
            for sg in range(nseg):
                rows_r.append(cre)
                rows_i.append(cim)
                sq = b * nseg + sg
                cre, cim = (er[s][sq:sq + 1] + asr * cre - asi * cim,
                            ei[s][sq:sq + 1] + asr * cim + asi * cre)
        cr.append(jnp.concatenate(rows_r, axis=0))
        ci.append(jnp.concatenate(rows_i, axis=0))

    def fix(k, carry):
        wr, wi = carry
        base = pl.multiple_of(k * nseq, nseq)
        nr, ni = [], []
        for s in range(nslab):
            xre_ref[s, pl.ds(base, nseq), :] += wr[s]
            xim_ref[s, pl.ds(base, nseq), :] += wi[s]
            nr.append(ar[s] * wr[s] - ai[s] * wi[s])
            ni.append(ar[s] * wi[s] + ai[s] * wr[s])
        return tuple(nr), tuple(ni)

    lax.fori_loop(0, nk, fix, (tuple(cr), tuple(ci)), unroll=2)

    for s in range(nslab):
        for sq in range(nseq):
            rs = slice(sq * nk, (sq + 1) * nk)
            xn_ref[rs, s * LANES:(s + 1) * LANES] = xre_ref[s, pl.ds(sq, nk, stride=nseq), :].astype(BF16)
            xn_ref[rs, half + s * LANES:half + (s + 1) * LANES] = (
                xim_ref[s, pl.ds(sq, nk, stride=nseq), :].astype(BF16))
    xn = xn_ref[...]
    d = d_ref[...]
    for c in range(0, q * LANES, step_n):
        yy = yi_ref[:, c:c + step_n] + _dot(xn, cd_ref[:, c:c + step_n])
        for hh in range(step_n // LANES):
            j = c // LANES + hh
            y_ref[pl.ds(j, rows, stride=q), :] = (yy[:, hh * LANES:(hh + 1) * LANES]
                                                  + d * u_ref[pl.ds(j, rows, stride=q), :])


def _ssm(u2, a_re, a_im, log_dt, b_re, b_im, c_re, c_im, d_skip, bsz):
    t, sw = u2.shape
    q = SSM_CHUNK
    rows = t // q
    assert SUBLANES % bsz == 0 and rows % SUBLANES == 0
    nk = rows // SUBLANES
    tmat, bd, cd, aq, aseg = _ssm_matrices(a_re, a_im, log_dt, b_re, b_im, c_re, c_im, nk)
    nb, _, st = bd.shape
    half = st // 2
    nslab = half // LANES
    mat = lambda shape: pl.BlockSpec((None,) + shape, lambda i: (i, 0, 0))
    return pl.pallas_call(
        functools.partial(_ssm_kernel, bsz=bsz, nk=nk),
        out_shape=jax.ShapeDtypeStruct((t, sw), F32),
        grid=(nb,),
        in_specs=[pl.BlockSpec((t, LANES), lambda i: (0, i)),
                  mat((q * LANES, q * LANES)), mat((q * LANES, st)), mat((st, q * LANES)),
                  mat((1, half)), mat((1, half)), mat((1, half)), mat((1, half)),
                  pl.BlockSpec((1, LANES), lambda i: (0, i))],
        out_specs=pl.BlockSpec((t, LANES), lambda i: (0, i)),
        scratch_shapes=[pltpu.VMEM((rows, q * LANES), BF16), pltpu.VMEM((rows, q * LANES), F32)]
                       + [pltpu.VMEM((nslab, rows, LANES), F32) for _ in range(4)]
                       + [pltpu.VMEM((rows, st), BF16)],
        compiler_params=_cparams("arbitrary"),
        name="s5_scan",
    )(u2, tmat, bd, cd, aq[0], aq[1], aseg[0], aseg[1],
      d_skip.astype(F32).reshape(1, sw))


def _mix_kernel(x_ref, attn_ref, y_ref, wglu_ref, bglu_ref, wout_ref, gate_ref,
                g_ref, shift_ref, scale_ref, x1_ref, h2_ref):
    aw = attn_ref.shape[1]
    yg = jax.nn.gelu(y_ref[...])
    z = _dot(yg.astype(BF16), wglu_ref[...]) + bglu_ref[...]
    yy = yg * jax.nn.sigmoid(z)
    mix = _dot(attn_ref[...], wout_ref[:aw, :]) + _dot(yy.astype(BF16), wout_ref[aw:, :])
    x1 = x_ref[...] + gate_ref[...] * mix
    x1_ref[...] = x1
    h2_ref[...] = _norm_modulate(x1, g_ref[...], shift_ref[...], scale_ref[...]).astype(BF16)


def _mix(x2, attn, y, wglu_bf, bglu, wout_bf, mod4, norm_g, seq, tm):
    t, d = x2.shape
    aw = attn.shape[1]
    tiles_per_seq = seq // tm
    bidx = lambda i: i // tiles_per_seq
    modspec = lambda k: pl.BlockSpec((None, None, 1, d), lambda i: (bidx(i), k, 0, 0))
    return pl.pallas_call(
        _mix_kernel,
        out_shape=(jax.ShapeDtypeStruct((t, d), F32), jax.ShapeDtypeStruct((t, d), BF16)),
        grid=(t // tm,),
        in_specs=[pl.BlockSpec((tm, d), lambda i: (i, 0)),
                  pl.BlockSpec((tm, aw), lambda i: (i, 0)),
                  pl.BlockSpec((tm, aw), lambda i: (i, 0)),
                  _resident(wglu_bf.shape),
                  pl.BlockSpec((1, aw), lambda i: (0, 0)),
                  _resident(wout_bf.shape),
                  modspec(2),
                  pl.BlockSpec((1, d), lambda i: (0, 0)),
                  modspec(3), modspec(4)],
        out_specs=(pl.BlockSpec((tm, d), lambda i: (i, 0)), pl.BlockSpec((tm, d), lambda i: (i, 0))),
        compiler_params=_cparams("arbitrary"),
        name="mix_out_proj",
    )(x2, attn, y, wglu_bf, bglu.reshape(1, aw), wout_bf, mod4, norm_g.reshape(1, d), mod4, mod4)


def _ffn_kernel(h_ref, halo_ref, wa_ref, wg_ref, cw_ref, cb_ref, wd_ref, x1_ref, gate_ref,
                o_ref, a0_ref, a1_ref, g0_ref, g1_ref, acc_ref, *, tiles_per_seq, nf, units):
    n = pl.program_id(0)
    slot = n % 2
    prev = jnp.maximum(n - 1, 0)
    f_prev = prev % nf
    tm = h_ref.shape[0]

    @pl.when(n == 0)
    def _():
        a1_ref[...] = jnp.zeros(a1_ref.shape, F32)
        g1_ref[...] = jnp.zeros(g1_ref.shape, F32)

    @pl.when(f_prev == 0)
    def _():
        acc_ref[...] = jnp.zeros(acc_ref.shape, F32)

    row_tile = jnp.minimum(n, units - 1) // nf

    def stages(a_rd, g_rd, a_wr, g_wr):
        halo = jnp.where(row_tile % tiles_per_seq == 0, jnp.zeros_like(halo_ref[...]), halo_ref[...])
        h = h_ref[...]
        a_wr[...] = _dot(jnp.concatenate([halo, h], axis=0), wa_ref[...])
        g_wr[...] = _dot(h, wg_ref[...])
        a_ext = a_rd[...]
        cw = cw_ref[...]
        conv = (cw[2:3] * a_ext[CONV_HALO:] + cw[1:2] * a_ext[CONV_HALO - 1:CONV_HALO - 1 + tm]
                + cw[0:1] * a_ext[CONV_HALO - 2:CONV_HALO - 2 + tm] + cb_ref[...])
        act = (jax.nn.gelu(conv) * g_rd[...]).astype(BF16)
        acc_ref[...] += _dot(act, wd_ref[...])

    @pl.when(slot == 0)
    def _():
        stages(a1_ref, g1_ref, a0_ref, g0_ref)

    @pl.when(slot == 1)
    def _():
        stages(a0_ref, g0_ref, a1_ref, g1_ref)

    @pl.when((f_prev == nf - 1) & (n > 0))
    def _():
        o_ref[...] = x1_ref[...] + gate_ref[...] * acc_ref[...]


def _ffn(h2, x1, wup_bf, conv_w, conv_b, wdown_bf, mod4, seq, tm, tf):
    t, d = h2.shape
    dff = wdown_bf.shape[0]
    nf = dff // tf
    tiles_per_seq = seq // tm
    halo_blocks = tm // CONV_HALO
    units = (t // tm) * nf
    cur_tile = lambda n: jnp.minimum(n, units - 1) // nf
    cur_blk = lambda n: jnp.minimum(n, units - 1) % nf
    prev_tile = lambda n: jnp.maximum(n - 1, 0) // nf
    prev_blk = lambda n: jnp.maximum(n - 1, 0) % nf
    return pl.pallas_call(
        functools.partial(_ffn_kernel, tiles_per_seq=tiles_per_seq, nf=nf, units=units),
        out_shape=jax.ShapeDtypeStruct((t, d), F32),
        grid=(units + 1,),
        in_specs=[pl.BlockSpec((tm, d), lambda n: (cur_tile(n), 0)),
                  pl.BlockSpec((CONV_HALO, d), lambda n: (jnp.maximum(cur_tile(n) * halo_blocks - 1, 0), 0)),
                  pl.BlockSpec((d, tf), lambda n: (0, cur_blk(n))),
                  pl.BlockSpec((d, tf), lambda n: (0, nf + cur_blk(n))),
                  pl.BlockSpec((conv_w.shape[0], tf), lambda n: (0, prev_blk(n))),
                  pl.BlockSpec((1, tf), lambda n: (0, prev_blk(n))),
                  pl.BlockSpec((tf, d), lambda n: (prev_blk(n), 0)),
                  pl.BlockSpec((tm, d), lambda n: (prev_tile(n), 0)),
                  pl.BlockSpec((None, None, 1, d), lambda n: (prev_tile(n) // tiles_per_seq, 5, 0, 0))],
        out_specs=pl.BlockSpec((tm, d), lambda n: (prev_tile(n), 0)),
        scratch_shapes=[pltpu.VMEM((tm + CONV_HALO, tf), F32), pltpu.VMEM((tm + CONV_HALO, tf), F32),
                        pltpu.VMEM((tm, tf), F32), pltpu.VMEM((tm, tf), F32),
                        pltpu.VMEM((tm, d), F32)],
        compiler_params=_cparams("arbitrary"),
        name="conv_ffn",
    )(h2, h2, wup_bf, wup_bf, conv_w, conv_b.reshape(1, dff), wdown_bf, x1, mod4)


def kernel(x, c, w_ada, b_ada, norm1_g, norm2_g, w_in, q_norm_g, k_norm_g, lambda_q1, lambda_k1,
           lambda_q2, lambda_k2, attn_sub_norm_g, ssm_a_re, ssm_a_im, ssm_log_dt, ssm_b_re, ssm_b_im,
           ssm_c_re, ssm_c_im, ssm_d, w_glu, b_glu, w_out, w_up, conv_w, conv_b, w_down):
    bsz, seq, d = x.shape
    depth = w_ada.shape[0]
    tm = min(ROW_TILE, seq)
    x2 = x.reshape(bsz * seq, d)
    for l in range(depth):
        lambda_init = 0.8 - 0.6 * math.exp(-0.3 * l)
        mod = _ada(c, w_ada[l], b_ada[l])
        mod4 = mod.reshape(bsz, 6, 1, d)
        qkv, u = _inproj(x2, mod4, norm1_g[l], w_in[l].astype(BF16), q_norm_g[l], k_norm_g[l], seq, tm)
        lamv = jnp.stack([lambda_q1[l], lambda_k1[l], lambda_q2[l], lambda_k2[l]]).astype(F32)
        attn = _attention(qkv, lamv, attn_sub_norm_g[l], bsz, seq, lambda_init, tm)
        y = _ssm(u, ssm_a_re[l], ssm_a_im[l], ssm_log_dt[l], ssm_b_re[l], ssm_b_im[l],
                 ssm_c_re[l], ssm_c_im[l], ssm_d[l], bsz)
        x1, h2 = _mix(x2, attn, y, w_glu[l].astype(BF16), b_glu[l], w_out[l].astype(BF16), mod4,
                      norm2_g[l], seq, tm)
        x2 = _ffn(h2, x1, w_up[l].astype(BF16), conv_w[l], conv_b[l], w_down[l].astype(BF16), mod4,
                  seq, tm, FFN_TF)
    return x2.reshape(bsz, seq, d)
```

```python
import functools
import math

import jax
import jax.numpy as jnp
from jax import lax
from jax.experimental import pallas as pl
from jax.experimental.pallas import tpu as pltpu

F32 = jnp.float32
BF16 = jnp.bfloat16

EPS = 1e-6
LANES = 128
SUBLANES = 8
MXU_WIDTH = 256
SSM_GROUP = 16
SSM_CHUNK = 8
CONV_HALO = 2 * SUBLANES
VMEM_LIMIT = 56 * 1024 * 1024
ROW_TILE = 512
FFN_TF = 512
FFN_PARTS = 4


def _cparams(*sem):
    return pltpu.CompilerParams(dimension_semantics=sem, vmem_limit_bytes=VMEM_LIMIT)


def _dot(a, b):
    return jnp.dot(a, b, preferred_element_type=F32)


def _dot_nt(a, b):
    return lax.dot_general(a, b, (((1,), (1,)), ((), ())), preferred_element_type=F32)


def _resident(shape):
    return pl.BlockSpec(shape, lambda *_: (0,) * len(shape), pipeline_mode=pl.Buffered(1))


def _ada_kernel(c_ref, w_ref, b_ref, o_ref):
    c = c_ref[...]
    o_ref[...] = _dot(c * jax.nn.sigmoid(c), w_ref[...]) + b_ref[...]


def _ada(c, w, b, tn=1024):
    bsz, d = c.shape
    n = w.shape[1]
    return pl.pallas_call(
        _ada_kernel,
        out_shape=jax.ShapeDtypeStruct((bsz, n), F32),
        grid=(n // tn,),
        in_specs=[pl.BlockSpec((bsz, d), lambda j: (0, 0)),
                  pl.BlockSpec((d, tn), lambda j: (0, j)),
                  pl.BlockSpec((1, tn), lambda j: (0, j))],
        out_specs=pl.BlockSpec((bsz, tn), lambda j: (0, j)),
        compiler_params=_cparams("arbitrary"),
        name="ada_ln",
    )(c, w, b.reshape(1, n))


def _norm_modulate(x, g, shift, scale):
    y = x * lax.rsqrt(jnp.mean(x * x, axis=-1, keepdims=True) + EPS)
    return y * g * (1.0 + scale) + shift


def _half_rms_norm(a, gain, post_scale):
    outs = []
    half = LANES // 2
    for cidx in range(a.shape[1] // LANES):
        blk = a[:, cidx * LANES:(cidx + 1) * LANES]
        sq = blk * blk
        lo = lax.broadcasted_iota(jnp.int32, blk.shape, 1) < half
        s_lo = jnp.sum(jnp.where(lo, sq, 0.0), axis=-1, keepdims=True)
        s_hi = jnp.sum(jnp.where(lo, 0.0, sq), axis=-1, keepdims=True)
        ms = jnp.where(lo, s_lo, s_hi) * (1.0 / half)
        outs.append(blk * lax.rsqrt(ms + EPS) * (gain * post_scale))
    return jnp.concatenate(outs, axis=-1)


def _inproj_kernel(x_ref, g_ref, shift_ref, scale_ref, w_ref, qg_ref, kg_ref,
                   qkv_ref, u_ref, *, q_scale):
    aw = u_ref.shape[1]
    h = _norm_modulate(x_ref[...], g_ref[...], shift_ref[...], scale_ref[...]).astype(BF16)
    qkv_ref[:, :aw] = _half_rms_norm(_dot(h, w_ref[:, :aw]), qg_ref[...], q_scale).astype(BF16)
    qkv_ref[:, aw:2 * aw] = _half_rms_norm(_dot(h, w_ref[:, aw:2 * aw]), kg_ref[...], 1.0).astype(BF16)
    qkv_ref[:, 2 * aw:] = _dot(h, w_ref[:, 2 * aw:3 * aw]).astype(BF16)
    u_ref[...] = _dot(h, w_ref[:, 3 * aw:])


def _inproj(x2, mod4, norm_g, w_bf, qg, kg, seq, tm):
    t, d = x2.shape
    aw = w_bf.shape[1] // 4
    tiles_per_seq = seq // tm
    bidx = lambda i: i // tiles_per_seq
    qg2 = jnp.tile(qg, 2).reshape(1, LANES)
    kg2 = jnp.tile(kg, 2).reshape(1, LANES)
    head_dim = qg.shape[0]
    return pl.pallas_call(
        functools.partial(_inproj_kernel, q_scale=head_dim ** -0.5 * math.log2(math.e)),
        out_shape=(jax.ShapeDtypeStruct((t, 3 * aw), BF16), jax.ShapeDtypeStruct((t, aw), F32)),
        grid=(t // tm,),
        in_specs=[pl.BlockSpec((tm, d), lambda i: (i, 0)),
                  pl.BlockSpec((1, d), lambda i: (0, 0)),
                  pl.BlockSpec((None, None, 1, d), lambda i: (bidx(i), 0, 0, 0)),
                  pl.BlockSpec((None, None, 1, d), lambda i: (bidx(i), 1, 0, 0)),
                  _resident(w_bf.shape),
                  pl.BlockSpec((1, LANES), lambda i: (0, 0)),
                  pl.BlockSpec((1, LANES), lambda i: (0, 0))],
        out_specs=(pl.BlockSpec((tm, 3 * aw), lambda i: (i, 0)),
                   pl.BlockSpec((tm, aw), lambda i: (i, 0))),
        compiler_params=_cparams("arbitrary"),
        name="in_proj",
    )(x2, norm_g.reshape(1, d), mod4, mod4, w_bf, qg2, kg2)


ATTN_ROWS = 64
NEG_BIG = -1e30
POS_SPLIT = 64


def _attn_kernel(lamv_ref, gsub_ref, coef_ref, pos_ref, q_ref, k_ref, v_ref, o_ref,
                 kaug_ref, vaug_ref, qaug_ref, s_ref, p_ref, m_ref, alpha_ref, acc_ref,
                 *, tq, lambda_init):
    i = pl.program_id(2)
    half = LANES // 2

    @pl.when(i == 0)
    def _():
        kaug_ref[:, :LANES] = k_ref[...]
        kaug_ref[:, LANES:] = pos_ref[...]
        vaug_ref[:, :LANES] = v_ref[...]
        vaug_ref[:, LANES:] = jnp.ones((v_ref.shape[0], LANES), BF16)

    q = q_ref[...]
    lane = lax.broadcasted_iota(jnp.int32, q.shape, 1)
    zero = jnp.zeros_like(q)
    qaug_ref[0, :, :LANES] = jnp.where(lane < half, q, zero)
    qaug_ref[1, :, :LANES] = jnp.where(lane >= half, q, zero)
    coef = jnp.broadcast_to(coef_ref[...], q.shape)
    qaug_ref[0, :, LANES:] = coef
    qaug_ref[1, :, LANES:] = coef

    m_ref[...] = jnp.full(m_ref.shape, NEG_BIG, F32)
    acc_ref[...] = jnp.zeros(acc_ref.shape, F32)
    alpha_ref[1] = jnp.ones(alpha_ref.shape[1:], F32)
    p_ref[1] = jnp.zeros(p_ref.shape[1:], BF16)

    nsub = tq // ATTN_ROWS

    hq = tq // 2

    def scores(j, idx, part):
        start = pl.multiple_of(j * tq + part * hq, hq)
        s_ref[idx, :, part * hq:(part + 1) * hq] = _dot_nt(qaug_ref[idx], kaug_ref[pl.ds(start, hq), :])

    def softmax(idx, diagonal, chunks):
        for r in chunks:
            rows = slice(r * ATTN_ROWS, (r + 1) * ATTN_ROWS)
            width = min(tq, -(-((r + 1) * ATTN_ROWS) // LANES) * LANES) if diagonal else tq
            s = s_ref[idx, rows, :width]
            if diagonal:
                last = s[:, width - LANES:]
                row = lax.broadcasted_iota(jnp.int32, last.shape, 0) + r * ATTN_ROWS
                col = lax.broadcasted_iota(jnp.int32, last.shape, 1) + (width - LANES)
                last = jnp.where(col <= row, last, NEG_BIG)
                s = last if width == LANES else jnp.concatenate([s[:, :width - LANES], last], axis=-1)
            m_old = m_ref[idx, rows, :]
            m_new = jnp.maximum(m_old, jnp.max(s, axis=-1, keepdims=True))
            alpha_ref[idx, rows, :] = jnp.exp2(m_old - m_new)
            m_ref[idx, rows, :] = m_new
            p_ref[idx, rows, :width] = jnp.exp2(s - jnp.tile(m_new, (1, width // LANES))).astype(BF16)
            if width < tq:
                p_ref[idx, rows, width:] = jnp.zeros((ATTN_ROWS, tq - width), BF16)

    def values(j, idx):
        start = pl.multiple_of(jnp.maximum(j, 0) * tq, tq)
        pv = _dot(p_ref[idx], vaug_ref[pl.ds(start, tq), :])
        acc_ref[idx] = jnp.tile(alpha_ref[idx], (1, 2)) * acc_ref[idx] + pv

    def unit(j_s, idx_s, idx_sm, diagonal, j_v, idx_v):
        matmuls = []
        if j_s is not None:
            matmuls += [functools.partial(scores, j_s, idx_s, part) for part in range(2)]
        matmuls.append(functools.partial(values, j_v, idx_v))
        bounds = [round(k * nsub / len(matmuls)) for k in range(len(matmuls) + 1)]
        for k, matmul in enumerate(matmuls):
            matmul()
            softmax(idx_sm, diagonal, range(bounds[k], bounds[k + 1]))

    scores(0, 0, 0)
    scores(0, 0, 1)

    def block(j):
        unit(j, 1, 0, False, j - 1, 1)
        unit(j + 1, 0, 1, False, j, 0)

    def two_blocks(jj, carry):
        block(2 * jj)
        block(2 * jj + 1)
        return carry

    lax.fori_loop(0, i // 2, two_blocks, 0)

    @pl.when(i % 2 == 1)
    def _():
        block(i - 1)

    unit(i, 1, 0, True, i - 1, 1)
    unit(None, None, 1, True, i, 0)
    values(i, 1)

    lv = lamv_ref[...]
    lam = (jnp.exp(jnp.sum(lv[0:1] * lv[1:2], axis=-1, keepdims=True))
           - jnp.exp(jnp.sum(lv[2:3] * lv[3:4], axis=-1, keepdims=True)) + lambda_init)
    a0 = acc_ref[0]
    a1 = acc_ref[1]
    o = a0[:, :LANES] / a0[:, LANES:] - lam * (a1[:, :LANES] / a1[:, LANES:])
    o = o * lax.rsqrt(jnp.mean(o * o, axis=-1, keepdims=True) + EPS)
    o_ref[...] = (o * gsub_ref[...] * (1.0 - lambda_init)).astype(BF16)


def _split_bf16(x, n):
    parts = []
    for _ in range(n):
        p = x.astype(BF16)
        parts.append(p)
        x = x - p.astype(F32)
    return parts


def _attention(qkv, lamv, gsub, bsz, seq, lambda_init, tq):
    t = qkv.shape[0]
    aw = qkv.shape[1] // 3
    heads = aw // LANES
    nq = seq // tq
    c = jnp.asarray([2.0 ** (-8.0 * (i + 1) / heads) * math.log2(math.e) for i in range(heads)], F32)
    cparts = _split_bf16(c, 3)
    coef = jnp.stack([p * POS_SPLIT for p in cparts] + cparts, axis=-1)
    coef = jnp.pad(coef, ((0, 0), (0, LANES - coef.shape[1]))).reshape(heads, 1, LANES)
    kpos = jnp.arange(seq, dtype=jnp.int32)
    hi = (kpos // POS_SPLIT).astype(BF16)
    lo = (kpos % POS_SPLIT).astype(BF16)
    pos = jnp.pad(jnp.stack([hi, hi, hi, lo, lo, lo], axis=-1), ((0, 0), (0, LANES - 6)))
    return pl.pallas_call(
        functools.partial(_attn_kernel, tq=tq, lambda_init=lambda_init),
        out_shape=jax.ShapeDtypeStruct((t, aw), BF16),
        grid=(bsz, heads, nq),
        in_specs=[pl.BlockSpec((4, LANES // 2), lambda b, h, i: (0, 0)),
                  pl.BlockSpec((1, LANES), lambda b, h, i: (0, 0)),
                  pl.BlockSpec((None, 1, LANES), lambda b, h, i: (h, 0, 0)),
                  pl.BlockSpec((seq, LANES), lambda b, h, i: (0, 0)),
                  pl.BlockSpec((tq, LANES), lambda b, h, i: (b * nq + i, h)),
                  pl.BlockSpec((seq, LANES), lambda b, h, i: (b, heads + h)),
                  pl.BlockSpec((seq, LANES), lambda b, h, i: (b, 2 * heads + h))],
        out_specs=pl.BlockSpec((tq, LANES), lambda b, h, i: (b * nq + i, h)),
        scratch_shapes=[pltpu.VMEM((seq, 2 * LANES), BF16), pltpu.VMEM((seq, 2 * LANES), BF16),
                        pltpu.VMEM((2, tq, 2 * LANES), BF16),
                        pltpu.VMEM((2, tq, tq), F32), pltpu.VMEM((2, tq, tq), BF16),
                        pltpu.VMEM((2, tq, LANES), F32), pltpu.VMEM((2, tq, LANES), F32),
                        pltpu.VMEM((2, tq, 2 * LANES), F32)],
        compiler_params=_cparams("arbitrary", "arbitrary", "arbitrary"),
        name="diff_attn",
    )(lamv, gsub.reshape(1, LANES), coef, pos, qkv, qkv, qkv)


def _ssm_matrix_kernel(c_re_ref, c_im_ref, bt_re_ref, bt_im_ref, pw_re_ref, pw_im_ref,
                       tm_ref, bd_ref, cd_ref):
    q = SSM_CHUNK
    nrow, p = c_re_ref.shape
    width = pw_re_ref.shape[1]
    gb = width // p
    row_g = lax.broadcasted_iota(jnp.int32, (nrow, width), 0) // (nrow // gb)
    col_g = lax.broadcasted_iota(jnp.int32, (nrow, width), 1) // p
    same_group = row_g == col_g

    def expand(ref):
        return jnp.where(same_group, jnp.tile(ref[...], (1, gb)), 0.0)

    def times_power(xr, xi, n):
        pr = pw_re_ref[n:n + 1, :]
        pi = pw_im_ref[n:n + 1, :]
        return xr * pr - xi * pi, xr * pi + xi * pr

    def dot_nt_f32(a, b):
        return lax.dot_general(a, b, (((1,), (1,)), ((), ())), precision=lax.Precision.HIGHEST,
                               preferred_element_type=F32)

    c_re, c_im = expand(c_re_ref), expand(c_im_ref)
    bt_re, bt_im = expand(bt_re_ref), expand(bt_im_ref)

    tm_ref[...] = jnp.zeros(tm_ref.shape, BF16)
    for tau in range(q):
        ca_re, ca_im = times_power(c_re, c_im, tau)
        blk = (dot_nt_f32(bt_re, ca_re) - dot_nt_f32(bt_im, ca_im)).astype(BF16)
        for jp in range(q - tau):
            tm_ref[jp * LANES:(jp + 1) * LANES, (jp + tau) * LANES:(jp + tau + 1) * LANES] = blk
    for jp in range(q):
        e_re, e_im = times_power(bt_re, bt_im, q - 1 - jp)
        bd_ref[jp * LANES:(jp + 1) * LANES, :width] = e_re.astype(BF16)
        bd_ref[jp * LANES:(jp + 1) * LANES, width:] = e_im.astype(BF16)
    for j in range(q):
        f_re, f_im = times_power(c_re, c_im, j + 1)
        cd_ref[:width, j * LANES:(j + 1) * LANES] = f_re.T.astype(BF16)
        cd_ref[width:, j * LANES:(j + 1) * LANES] = (-f_im).T.astype(BF16)


def _ssm_matrices(a_re, a_im, log_dt, b_re, b_im, c_re, c_im, nk):
    g, p = a_re.shape
    cch = b_re.shape[-1]
    q = SSM_CHUNK
    gb = LANES // cch
    nb = g // gb
    dt = jnp.exp(log_dt.astype(F32))[:, None]
    ar, ai = a_re.astype(F32), a_im.astype(F32)
    mag = jnp.exp(dt * ar)
    abar_re, abar_im = mag * jnp.cos(dt * ai), mag * jnp.sin(dt * ai)
    den = ar * ar + ai * ai
    nr, ni = abar_re - 1.0, abar_im
    coef_re = (nr * ar + ni * ai) / den
    coef_im = (ni * ar - nr * ai) / den
    br, bi = b_re.astype(F32), b_im.astype(F32)
    bbar_re = coef_re[..., None] * br - coef_im[..., None] * bi
    bbar_im = coef_re[..., None] * bi + coef_im[..., None] * br

    def power(n):
        n = jnp.asarray(n, F32)[:, None, None]
        m = jnp.exp(n * (dt * ar)[None])
        return m * jnp.cos(n * (dt * ai)[None]), m * jnp.sin(n * (dt * ai)[None])

    rows_c = lambda x: x.reshape(g * cch, p)
    c2 = [rows_c(c_re.astype(F32)), rows_c(c_im.astype(F32))]
    bt2 = [rows_c(bbar_re.transpose(0, 2, 1)), rows_c(bbar_im.transpose(0, 2, 1))]
    pw = [x.reshape(q + 1, g * p) for x in power(range(q + 1))]
    width = gb * p
    side = q * LANES
    par = pl.BlockSpec((LANES, p), lambda i: (i, 0))
    pws = pl.BlockSpec((q + 1, width), lambda i: (0, i))
    mat = lambda r, c: pl.BlockSpec((None, r, c), lambda i: (i, 0, 0))
    tmat, bd, cd = pl.pallas_call(
        _ssm_matrix_kernel,
        out_shape=(jax.ShapeDtypeStruct((nb, side, side), BF16),
                   jax.ShapeDtypeStruct((nb, side, 2 * width), BF16),
                   jax.ShapeDtypeStruct((nb, 2 * width, side), BF16)),
        grid=(nb,),
        in_specs=[par, par, par, par, pws, pws],
        out_specs=(mat(side, side), mat(side, 2 * width), mat(2 * width, side)),
        compiler_params=_cparams("arbitrary"),
        name="s5_matrices",
    )(c2[0], c2[1], bt2[0], bt2[1], pw[0], pw[1])

    def lanes(x):
        return x.reshape(x.shape[0], nb, width).transpose(1, 0, 2)

    aq = [lanes(x) for x in power([q])]
    aseg = [lanes(x) for x in power([q * nk])]
    return tmat, bd, cd, aq, aseg


def _ssm_kernel(u_ref, tm_ref, bd_ref, cd_ref, aqr_ref, aqi_ref, asr_ref, asi_ref,
                d_ref, y_ref, z_ref, yi_ref, sre_ref, sim_ref, xre_ref, xim_ref, xn_ref, *, bsz, nk):
    q = SSM_CHUNK
    rows = z_ref.shape[0]
    nslab = sre_ref.shape[0]
    half = nslab * LANES
    nseq = rows // nk
    nseg = nseq // bsz
    step_n = MXU_WIDTH

    for j in range(q):
        z_ref[:, j * LANES:(j + 1) * LANES] = u_ref[pl.ds(j, rows, stride=q), :].astype(BF16)
    z = z_ref[...]
    for c in range(0, q * LANES, step_n):
        yi_ref[:, c:c + step_n] = _dot(z, tm_ref[:, c:c + step_n])
    for c in range(0, 2 * half, step_n):
        s = _dot(z, bd_ref[:, c:c + step_n])
        dst = sre_ref if c < half else sim_ref
        for hh in range(step_n // LANES):
            slab = (c % half) // LANES + hh
            for sq in range(nseq):
                dst[slab, pl.ds(sq, nk, stride=nseq), :] = s[sq * nk:(sq + 1) * nk, hh * LANES:(hh + 1) * LANES]

    ar = [jnp.broadcast_to(aqr_ref[:, s * LANES:(s + 1) * LANES], (nseq, LANES)) for s in range(nslab)]
    ai = [jnp.broadcast_to(aqi_ref[:, s * LANES:(s + 1) * LANES], (nseq, LANES)) for s in range(nslab)]

    def step(k, carry):
        xr, xi = carry
        base = pl.multiple_of(k * nseq, nseq)
        nr, ni = [], []
        for s in range(nslab):
            xre_ref[s, pl.ds(base, nseq), :] = xr[s]
            xim_ref[s, pl.ds(base, nseq), :] = xi[s]
            nr.append(ar[s] * xr[s] - ai[s] * xi[s] + sre_ref[s, pl.ds(base, nseq), :])
            ni.append(ar[s] * xi[s] + ai[s] * xr[s] + sim_ref[s, pl.ds(base, nseq), :])
        return tuple(nr), tuple(ni)

    zero = tuple(jnp.zeros((nseq, LANES), F32) for _ in range(nslab))
    er, ei = lax.fori_loop(0, nk, step, (zero, zero), unroll=2)

    cr, ci = [], []
    for s in range(nslab):
        asr = asr_ref[:, s * LANES:(s + 1) * LANES]
        asi = asi_ref[:, s * LANES:(s + 1) * LANES]
        rows_r, rows_i = [], []
        for b in range(bsz):
            cre = jnp.zeros((1, LANES), F32)
            cim = jnp.zeros((1, LANES), F32)
            for sg in range(nseg):
                rows_r.append(cre)
                rows_i.append(cim)
                sq = b * nseg + sg
                cre, cim = (er[s][sq:sq + 1] + asr * cre - asi * cim,
                            ei[s][sq:sq + 1] + asr * cim + asi * cre)
        cr.append(jnp.concatenate(rows_r, axis=0))
        ci.append(jnp.concatenate(rows_i, axis=0))

    def fix(k, carry):
        wr, wi = carry
        base = pl.multiple_of(k * nseq, nseq)
        nr, ni = [], []
        for s in range(nslab):
            xre_ref[s, pl.ds(base, nseq), :] += wr[s]
            xim_ref[s, pl.ds(base, nseq), :] += wi[s]
            nr.append(ar[s] * wr[s] - ai[s] * wi[s])
            ni.append(ar[s] * wi[s] + ai[s] * wr[s])
        return tuple(nr), tuple(ni)

    lax.fori_loop(0, nk, fix, (tuple(cr), tuple(ci)), unroll=2)

    for s in range(nslab):
        for sq in range(nseq):
            rs = slice(sq * nk, (sq + 1) * nk)
            xn_ref[rs, s * LANES:(s + 1) * LANES] = xre_ref[s, pl.ds(sq, nk, stride=nseq), :].astype(BF16)
            xn_ref[rs, half + s * LANES:half + (s + 1) * LANES] = (
                xim_ref[s, pl.ds(sq, nk, stride=nseq), :].astype(BF16))
    xn = xn_ref[...]
    d = d_ref[...]
    for c in range(0, q * LANES, step_n):
        yy = yi_ref[:, c:c + step_n] + _dot(xn, cd_ref[:, c:c + step_n])
        for hh in range(step_n // LANES):
            j = c // LANES + hh
            y_ref[pl.ds(j, rows, stride=q), :] = (yy[:, hh * LANES:(hh + 1) * LANES]
                                                  + d * u_ref[pl.ds(j, rows, stride=q), :])


def _ssm(u2, a_re, a_im, log_dt, b_re, b_im, c_re, c_im, d_skip, bsz):
    t, sw = u2.shape
    q = SSM_CHUNK
    rows = t // q
    assert SUBLANES % bsz == 0 and rows % SUBLANES == 0
    nk = rows // SUBLANES
    tmat, bd, cd, aq, aseg = _ssm_matrices(a_re, a_im, log_dt, b_re, b_im, c_re, c_im, nk)
    nb, _, st = bd.shape
    half = st // 2
    nslab = half // LANES
    mat = lambda shape: pl.BlockSpec((None,) + shape, lambda i: (i, 0, 0))
    return pl.pallas_call(
        functools.partial(_ssm_kernel, bsz=bsz, nk=nk),
        out_shape=jax.ShapeDtypeStruct((t, sw), F32),
        grid=(nb,),
        in_specs=[pl.BlockSpec((t, LANES), lambda i: (0, i)),
                  mat((q * LANES, q * LANES)), mat((q * LANES, st)), mat((st, q * LANES)),
                  mat((1, half)), mat((1, half)), mat((1, half)), mat((1, half)),
                  pl.BlockSpec((1, LANES), lambda i: (0, i))],
        out_specs=pl.BlockSpec((t, LANES), lambda i: (0, i)),
        scratch_shapes=[pltpu.VMEM((rows, q * LANES), BF16), pltpu.VMEM((rows, q * LANES), F32)]
                       + [pltpu.VMEM((nslab, rows, LANES), F32) for _ in range(4)]
                       + [pltpu.VMEM((rows, st), BF16)],
        compiler_params=_cparams("arbitrary"),
        name="s5_scan",
    )(u2, tmat, bd, cd, aq[0], aq[1], aseg[0], aseg[1],
      d_skip.astype(F32).reshape(1, sw))


def _mix_kernel(x_ref, attn_ref, y_ref, wglu_ref, bglu_ref, wout_ref, gate_ref,
                g_ref, shift_ref, scale_ref, x1_ref, h2_ref):
    aw = attn_ref.shape[1]
    yg = jax.nn.gelu(y_ref[...])
    z = _dot(yg.astype(BF16), wglu_ref[...]) + bglu_ref[...]
    yy = yg * jax.nn.sigmoid(z)
    mix = _dot(attn_ref[...], wout_ref[:aw, :]) + _dot(yy.astype(BF16), wout_ref[aw:, :])
    x1 = x_ref[...] + gate_ref[...] * mix
    x1_ref[...] = x1
    h2_ref[...] = _norm_modulate(x1, g_ref[...], shift_ref[...], scale_ref[...]).astype(BF16)


def _mix(x2, attn, y, wglu_bf, bglu, wout_bf, mod4, norm_g, seq, tm):
    t, d = x2.shape
    aw = attn.shape[1]
    tiles_per_seq = seq // tm
    bidx = lambda i: i // tiles_per_seq
    modspec = lambda k: pl.BlockSpec((None, None, 1, d), lambda i: (bidx(i), k, 0, 0))
    return pl.pallas_call(
        _mix_kernel,
        out_shape=(jax.ShapeDtypeStruct((t, d), F32), jax.ShapeDtypeStruct((t, d), BF16)),
        grid=(t // tm,),
        in_specs=[pl.BlockSpec((tm, d), lambda i: (i, 0)),
                  pl.BlockSpec((tm, aw), lambda i: (i, 0)),
                  pl.BlockSpec((tm, aw), lambda i: (i, 0)),
                  _resident(wglu_bf.shape),
                  pl.BlockSpec((1, aw), lambda i: (0, 0)),
                  _resident(wout_bf.shape),
                  modspec(2),
                  pl.BlockSpec((1, d), lambda i: (0, 0)),
                  modspec(3), modspec(4)],
        out_specs=(pl.BlockSpec((tm, d), lambda i: (i, 0)), pl.BlockSpec((tm, d), lambda i: (i, 0))),
        compiler_params=_cparams("arbitrary"),
        name="mix_out_proj",
    )(x2, attn, y, wglu_bf, bglu.reshape(1, aw), wout_bf, mod4, norm_g.reshape(1, d), mod4, mod4)


def _ffn_kernel(h_ref, halo_ref, wa_ref, wg_ref, cw_ref, cb_ref, wd_ref, x1_ref, gate_ref,
                o_ref, a0_ref, a1_ref, g0_ref, g1_ref, act0_ref, act1_ref, acc_ref,
                *, tiles_per_seq, nf, units):
    n = pl.program_id(0)
    slot = n % 2
    f_down = jnp.maximum(n - 2, 0) % nf
    tm = h_ref.shape[0]

    @pl.when(n == 0)
    def _():
        a1_ref[...] = jnp.zeros(a1_ref.shape, F32)
        g1_ref[...] = jnp.zeros(g1_ref.shape, F32)
        act0_ref[...] = jnp.zeros(act0_ref.shape, BF16)

    @pl.when(f_down == 0)
    def _():
        acc_ref[...] = jnp.zeros(acc_ref.shape, F32)

    row_tile = jnp.minimum(n, units - 1) // nf

    def stages(a_rd, g_rd, a_wr, g_wr, act_rd, act_wr):
        halo = jnp.where(row_tile % tiles_per_seq == 0, jnp.zeros_like(halo_ref[...]), halo_ref[...])
        h = h_ref[...]
        h_ext = jnp.concatenate([halo, h], axis=0)
        act = act_rd[...]
        cw = cw_ref[...]
        cb = cb_ref[...]
        tf = a_wr.shape[1]
        d = acc_ref.shape[1]

        def down(c):
            cols = slice(c * (d // FFN_PARTS), (c + 1) * (d // FFN_PARTS))
            acc_ref[:, cols] += _dot(act, wd_ref[:, cols])

        def up(c, lhs, w_ref, dst):
            cols = slice(c * (2 * tf // FFN_PARTS), (c + 1) * (2 * tf // FFN_PARTS))
            dst[:, cols] = _dot(lhs, w_ref[:, cols])

        def conv_gelu(r):
            nrow = tm // (2 * FFN_PARTS)
            r0 = CONV_HALO + r * nrow
            conv = (cw[2:3] * a_rd[r0:r0 + nrow, :] + cw[1:2] * a_rd[r0 - 1:r0 - 1 + nrow, :]
                    + cw[0:1] * a_rd[r0 - 2:r0 - 2 + nrow, :] + cb)
            act_wr[r * nrow:(r + 1) * nrow, :] = (
                jax.nn.gelu(conv) * g_rd[r * nrow:(r + 1) * nrow, :]).astype(BF16)

        mxu_work = ([functools.partial(down, c) for c in range(2)]
                    + [functools.partial(up, 0, h_ext, wa_ref, a_wr), functools.partial(down, 2),
                       functools.partial(up, 0, h, wg_ref, g_wr), functools.partial(down, 3),
                       functools.partial(up, 1, h_ext, wa_ref, a_wr), functools.partial(up, 1, h, wg_ref, g_wr)])
        for r, matmul in enumerate(mxu_work):
            matmul()
            conv_gelu(r)

    @pl.when(slot == 0)
    def _():
        stages(a1_ref, g1_ref, a0_ref, g0_ref, act0_ref, act1_ref)

    @pl.when(slot == 1)
    def _():
        stages(a0_ref, g0_ref, a1_ref, g1_ref, act1_ref, act0_ref)

    @pl.when((f_down == nf - 1) & (n > 1))
    def _():
        o_ref[...] = x1_ref[...] + gate_ref[...] * acc_ref[...]


def _ffn(h2, x1, wup_bf, conv_w, conv_b, wdown_bf, mod4, seq, tm, tf):
    t, d = h2.shape
    dff = wdown_bf.shape[0]
    nf = dff // tf
    tiles_per_seq = seq // tm
    halo_blocks = tm // CONV_HALO
    units = (t // tm) * nf
    cur_tile = lambda n: jnp.minimum(n, units - 1) // nf
    cur_blk = lambda n: jnp.minimum(n, units - 1) % nf
    conv_blk = lambda n: jnp.clip(n - 1, 0, units - 1) % nf
    down_tile = lambda n: jnp.maximum(n - 2, 0) // nf
    down_blk = lambda n: jnp.maximum(n - 2, 0) % nf
    return pl.pallas_call(
        functools.partial(_ffn_kernel, tiles_per_seq=tiles_per_seq, nf=nf, units=units),
        out_shape=jax.ShapeDtypeStruct((t, d), F32),
        grid=(units + 2,),
        in_specs=[pl.BlockSpec((tm, d), lambda n: (cur_tile(n), 0)),
                  pl.BlockSpec((CONV_HALO, d), lambda n: (jnp.maximum(cur_tile(n) * halo_blocks - 1, 0), 0)),
                  pl.BlockSpec((d, tf), lambda n: (0, cur_blk(n))),
                  pl.BlockSpec((d, tf), lambda n: (0, nf + cur_blk(n))),
                  pl.BlockSpec((conv_w.shape[0], tf), lambda n: (0, conv_blk(n))),
                  pl.BlockSpec((1, tf), lambda n: (0, conv_blk(n))),
                  pl.BlockSpec((tf, d), lambda n: (down_blk(n), 0)),
                  pl.BlockSpec((tm, d), lambda n: (down_tile(n), 0)),
                  pl.BlockSpec((None, None, 1, d), lambda n: (down_tile(n) // tiles_per_seq, 5, 0, 0))],
        out_specs=pl.BlockSpec((tm, d), lambda n: (down_tile(n), 0)),
        scratch_shapes=[pltpu.VMEM((tm + CONV_HALO, tf), F32), pltpu.VMEM((tm + CONV_HALO, tf), F32),
                        pltpu.VMEM((tm, tf), F32), pltpu.VMEM((tm, tf), F32),
                        pltpu.VMEM((tm, tf), BF16), pltpu.VMEM((tm, tf), BF16),
                        pltpu.VMEM((tm, d), F32)],
        compiler_params=_cparams("arbitrary"),
        name="conv_ffn",
    )(h2, h2, wup_bf, wup_bf, conv_w, conv_b.reshape(1, dff), wdown_bf, x1, mod4)


def kernel(x, c, w_ada, b_ada, norm1_g, norm2_g, w_in, q_norm_g, k_norm_g, lambda_q1, lambda_k1,
           lambda_q2, lambda_k2, attn_sub_norm_g, ssm_a_re, ssm_a_im, ssm_log_dt, ssm_b_re, ssm_b_im,
           ssm_c_re, ssm_c_im, ssm_d, w_glu, b_glu, w_out, w_up, conv_w, conv_b, w_down):
    bsz, seq, d = x.shape
    depth = w_ada.shape[0]
    tm = min(ROW_TILE, seq)
    x2 = x.reshape(bsz * seq, d)
    for l in range(depth):
        lambda_init = 0.8 - 0.6 * math.exp(-0.3 * l)
        mod = _ada(c, w_ada[l], b_ada[l])
        mod4 = mod.reshape(bsz, 6, 1, d)
        qkv, u = _inproj(x2, mod4, norm1_g[l], w_in[l].astype(BF16), q_norm_g[l], k_norm_g[l], seq, tm)
        lamv = jnp.stack([lambda_q1[l], lambda_k1[l], lambda_q2[l], lambda_k2[l]]).astype(F32)
        attn = _attention(qkv, lamv, attn_sub_norm_g[l], bsz, seq, lambda_init, tm)
        y = _ssm(u, ssm_a_re[l], ssm_a_im[l], ssm_log_dt[l], ssm_b_re[l], ssm_b_im[l],
                 ssm_c_re[l], ssm_c_im[l], ssm_d[l], bsz)
        x1, h2 = _mix(x2, attn, y, w_glu[l].astype(BF16), b_glu[l], w_out[l].astype(BF16), mod4,
                      norm2_g[l], seq, tm)
        x2 = _ffn(h2, x1, w_up[l].astype(BF16), conv_w[l], conv_b[l], w_down[l].astype(BF16), mod4,
                  seq, tm, FFN_TF)
    return x2.reshape(bsz, seq, d)
```

```python
import functools
import math

import jax
import jax.numpy as jnp
from jax import lax
from jax.experimental import pallas as pl
from jax.experimental.pallas import tpu as pltpu

F32 = jnp.float32
BF16 = jnp.bfloat16

EPS = 1e-6
LANES = 128
SUBLANES = 8
MXU_WIDTH = 256
SSM_GROUP = 16
SSM_CHUNK = 8
CONV_HALO = 2 * SUBLANES
VMEM_LIMIT = 56 * 1024 * 1024
ROW_TILE = 512
FFN_TF = 512
FFN_PARTS = 4


def _cparams(*sem):
    return pltpu.CompilerParams(dimension_semantics=sem, vmem_limit_bytes=VMEM_LIMIT)


def _dot(a, b):
    return jnp.dot(a, b, preferred_element_type=F32)


def _dot_nt(a, b):
    return lax.dot_general(a, b, (((1,), (1,)), ((), ())), preferred_element_type=F32)


def _resident(shape):
    return pl.BlockSpec(shape, lambda *_: (0,) * len(shape), pipeline_mode=pl.Buffered(1))


def _ada_kernel(c_ref, w_ref, b_ref, o_ref):
    c = c_ref[...]
    o_ref[...] = _dot(c * jax.nn.sigmoid(c), w_ref[...]) + b_ref[...]


def _ada(c, w, b, tn=1024):
    bsz, d = c.shape
    n = w.shape[1]
    return pl.pallas_call(
        _ada_kernel,
        out_shape=jax.ShapeDtypeStruct((bsz, n), F32),
        grid=(n // tn,),
        in_specs=[pl.BlockSpec((bsz, d), lambda j: (0, 0)),
                  pl.BlockSpec((d, tn), lambda j: (0, j)),
                  pl.BlockSpec((1, tn), lambda j: (0, j))],
        out_specs=pl.BlockSpec((bsz, tn), lambda j: (0, j)),
        compiler_params=_cparams("arbitrary"),
        name="ada_ln",
    )(c, w, b.reshape(1, n))


def _norm_modulate(x, g, shift, scale):
    y = x * lax.rsqrt(jnp.mean(x * x, axis=-1, keepdims=True) + EPS)
    return y * g * (1.0 + scale) + shift


def _half_rms_norm(a, gain, post_scale):
    outs = []
    half = LANES // 2
    for cidx in range(a.shape[1] // LANES):
        blk = a[:, cidx * LANES:(cidx + 1) * LANES]
        sq = blk * blk
        lo = lax.broadcasted_iota(jnp.int32, blk.shape, 1) < half
        s_lo = jnp.sum(jnp.where(lo, sq, 0.0), axis=-1, keepdims=True)
        s_hi = jnp.sum(jnp.where(lo, 0.0, sq), axis=-1, keepdims=True)
        ms = jnp.where(lo, s_lo, s_hi) * (1.0 / half)
        outs.append(blk * lax.rsqrt(ms + EPS) * (gain * post_scale))
    return jnp.concatenate(outs, axis=-1)


def _inproj_kernel(x_ref, g_ref, shift_ref, scale_ref, w_ref, qg_ref, kg_ref,
                   qkv_ref, u_ref, *, q_scale):
    aw = u_ref.shape[1]
    h = _norm_modulate(x_ref[...], g_ref[...], shift_ref[...], scale_ref[...]).astype(BF16)
    qkv_ref[:, :aw] = _half_rms_norm(_dot(h, w_ref[:, :aw]), qg_ref[...], q_scale).astype(BF16)
    qkv_ref[:, aw:2 * aw] = _half_rms_norm(_dot(h, w_ref[:, aw:2 * aw]), kg_ref[...], 1.0).astype(BF16)
    qkv_ref[:, 2 * aw:] = _dot(h, w_ref[:, 2 * aw:3 * aw]).astype(BF16)
    u_ref[...] = _dot(h, w_ref[:, 3 * aw:])


def _inproj(x2, mod4, norm_g, w_bf, qg, kg, seq, tm):
    t, d = x2.shape
    aw = w_bf.shape[1] // 4
    tiles_per_seq = seq // tm
    bidx = lambda i: i // tiles_per_seq
    qg2 = jnp.tile(qg, 2).reshape(1, LANES)
    kg2 = jnp.tile(kg, 2).reshape(1, LANES)
    head_dim = qg.shape[0]
    return pl.pallas_call(
        functools.partial(_inproj_kernel, q_scale=head_dim ** -0.5 * math.log2(math.e)),
        out_shape=(jax.ShapeDtypeStruct((t, 3 * aw), BF16), jax.ShapeDtypeStruct((t, aw), F32)),
        grid=(t // tm,),
        in_specs=[pl.BlockSpec((tm, d), lambda i: (i, 0)),
                  pl.BlockSpec((1, d), lambda i: (0, 0)),
                  pl.BlockSpec((None, None, 1, d), lambda i: (bidx(i), 0, 0, 0)),
                  pl.BlockSpec((None, None, 1, d), lambda i: (bidx(i), 1, 0, 0)),
                  _resident(w_bf.shape),
                  pl.BlockSpec((1, LANES), lambda i: (0, 0)),
                  pl.BlockSpec((1, LANES), lambda i: (0, 0))],
        out_specs=(pl.BlockSpec((tm, 3 * aw), lambda i: (i, 0)),
                   pl.BlockSpec((tm, aw), lambda i: (i, 0))),
        compiler_params=_cparams("arbitrary"),
        name="in_proj",
    )(x2, norm_g.reshape(1, d), mod4, mod4, w_bf, qg2, kg2)


ATTN_ROWS = 64
NEG_BIG = -1e30
POS_SPLIT = 64


def _attn_kernel(lamv_ref, gsub_ref, coef_ref, pos_ref, q_ref, k_ref, v_ref, o_ref,
                 kaug_ref, vaug_ref, qaug_ref, s_ref, p_ref, m_ref, alpha_ref, acc_ref,
                 *, tq, lambda_init):
    i = pl.program_id(2)
    half = LANES // 2

    @pl.when(i == 0)
    def _():
        kaug_ref[:, :LANES] = k_ref[...]
        kaug_ref[:, LANES:] = pos_ref[...]
        vaug_ref[:, :LANES] = v_ref[...]
        vaug_ref[:, LANES:] = jnp.ones((v_ref.shape[0], LANES), BF16)

    q = q_ref[...]
    lane = lax.broadcasted_iota(jnp.int32, q.shape, 1)
    zero = jnp.zeros_like(q)
    qaug_ref[0, :, :LANES] = jnp.where(lane < half, q, zero)
    qaug_ref[1, :, :LANES] = jnp.where(lane >= half, q, zero)
    coef = jnp.broadcast_to(coef_ref[...], q.shape)
    qaug_ref[0, :, LANES:] = coef
    qaug_ref[1, :, LANES:] = coef

    m_ref[...] = jnp.full(m_ref.shape, NEG_BIG, F32)
    acc_ref[...] = jnp.zeros(acc_ref.shape, F32)
    alpha_ref[1] = jnp.ones(alpha_ref.shape[1:], F32)
    p_ref[1] = jnp.zeros(p_ref.shape[1:], BF16)

    nsub = tq // ATTN_ROWS
    hq = tq // 2

    def scores(j, idx, part):
        start = pl.multiple_of(j * tq + part * hq, hq)
        s_ref[idx, :, part * hq:(part + 1) * hq] = _dot_nt(qaug_ref[idx], kaug_ref[pl.ds(start, hq), :])

    def softmax(idx, diagonal, chunks):
        for r in chunks:
            rows = slice(r * ATTN_ROWS, (r + 1) * ATTN_ROWS)
            s = s_ref[idx, rows, :]
            if diagonal:
                row = lax.broadcasted_iota(jnp.int32, s.shape, 0) + r * ATTN_ROWS
                col = lax.broadcasted_iota(jnp.int32, s.shape, 1)
                s = jnp.where(col <= row, s, NEG_BIG)
            m_old = m_ref[idx, rows, :]
            m_new = jnp.maximum(m_old, jnp.max(s, axis=-1, keepdims=True))
            alpha_ref[idx, rows, :] = jnp.exp2(m_old - m_new)
            m_ref[idx, rows, :] = m_new
            p_ref[idx, rows, :] = jnp.exp2(s - jnp.tile(m_new, (1, tq // LANES))).astype(BF16)

    def values(j, idx):
        start = pl.multiple_of(jnp.maximum(j, 0) * tq, tq)
        pv = _dot(p_ref[idx], vaug_ref[pl.ds(start, tq), :])
        acc_ref[idx] = jnp.tile(alpha_ref[idx], (1, 2)) * acc_ref[idx] + pv

    def unit(j_s, idx_s, idx_sm, diagonal, j_v, idx_v):
        matmuls = []
        if j_s is not None:
            matmuls += [functools.partial(scores, j_s, idx_s, part) for part in range(2)]
        matmuls.append(functools.partial(values, j_v, idx_v))
        bounds = [round(k * nsub / len(matmuls)) for k in range(len(matmuls) + 1)]
        for k, matmul in enumerate(matmuls):
            matmul()
            softmax(idx_sm, diagonal, range(bounds[k], bounds[k + 1]))

    scores(0, 0, 0)
    scores(0, 0, 1)

    def block(j):
        unit(j, 1, 0, False, j - 1, 1)
        unit(j + 1, 0, 1, False, j, 0)

    def two_blocks(jj, carry):
        block(2 * jj)
        block(2 * jj + 1)
        return carry

    lax.fori_loop(0, i // 2, two_blocks, 0)

    @pl.when(i % 2 == 1)
    def _():
        block(i - 1)

    unit(i, 1, 0, True, i - 1, 1)
    unit(None, None, 1, True, i, 0)
    values(i, 1)

    lv = lamv_ref[...]
    lam = (jnp.exp(jnp.sum(lv[0:1] * lv[1:2], axis=-1, keepdims=True))
           - jnp.exp(jnp.sum(lv[2:3] * lv[3:4], axis=-1, keepdims=True)) + lambda_init)
    a0 = acc_ref[0]
    a1 = acc_ref[1]
    o = a0[:, :LANES] / a0[:, LANES:] - lam * (a1[:, :LANES] / a1[:, LANES:])
    o = o * lax.rsqrt(jnp.mean(o * o, axis=-1, keepdims=True) + EPS)
    o_ref[...] = (o * gsub_ref[...] * (1.0 - lambda_init)).astype(BF16)


def _split_bf16(x, n):
    parts = []
    for _ in range(n):
        p = x.astype(BF16)
        parts.append(p)
        x = x - p.astype(F32)
    return parts


def _attention(qkv, lamv, gsub, bsz, seq, lambda_init, tq):
    t = qkv.shape[0]
    aw = qkv.shape[1] // 3
    heads = aw // LANES
    nq = seq // tq
    c = jnp.asarray([2.0 ** (-8.0 * (i + 1) / heads) * math.log2(math.e) for i in range(heads)], F32)
    cparts = _split_bf16(c, 3)
    coef = jnp.stack([p * POS_SPLIT for p in cparts] + cparts, axis=-1)
    coef = jnp.pad(coef, ((0, 0), (0, LANES - coef.shape[1]))).reshape(heads, 1, LANES)
    kpos = jnp.arange(seq, dtype=jnp.int32)
    hi = (kpos // POS_SPLIT).astype(BF16)
    lo = (kpos % POS_SPLIT).astype(BF16)
    pos = jnp.pad(jnp.stack([hi, hi, hi, lo, lo, lo], axis=-1), ((0, 0), (0, LANES - 6)))
    return pl.pallas_call(
        functools.partial(_attn_kernel, tq=tq, lambda_init=lambda_init),
        out_shape=jax.ShapeDtypeStruct((t, aw), BF16),
        grid=(bsz, heads, nq),
        in_specs=[pl.BlockSpec((4, LANES // 2), lambda b, h, i: (0, 0)),
                  pl.BlockSpec((1, LANES), lambda b, h, i: (0, 0)),
                  pl.BlockSpec((None, 1, LANES), lambda b, h, i: (h, 0, 0)),
                  pl.BlockSpec((seq, LANES), lambda b, h, i: (0, 0)),
                  pl.BlockSpec((tq, LANES), lambda b, h, i: (b * nq + i, h)),
                  pl.BlockSpec((seq, LANES), lambda b, h, i: (b, heads + h)),
                  pl.BlockSpec((seq, LANES), lambda b, h, i: (b, 2 * heads + h))],
        out_specs=pl.BlockSpec((tq, LANES), lambda b, h, i: (b * nq + i, h)),
        scratch_shapes=[pltpu.VMEM((seq, 2 * LANES), BF16), pltpu.VMEM((seq, 2 * LANES), BF16),
                        pltpu.VMEM((2, tq, 2 * LANES), BF16),
                        pltpu.VMEM((2, tq, tq), F32), pltpu.VMEM((2, tq, tq), BF16),
                        pltpu.VMEM((2, tq, LANES), F32), pltpu.VMEM((2, tq, LANES), F32),
                        pltpu.VMEM((2, tq, 2 * LANES), F32)],
        compiler_params=_cparams("arbitrary", "arbitrary", "arbitrary"),
        name="diff_attn",
    )(lamv, gsub.reshape(1, LANES), coef, pos, qkv, qkv, qkv)


def _ssm_matrix_kernel(c_re_ref, c_im_ref, bt_re_ref, bt_im_ref, pw_re_ref, pw_im_ref,
                       tm_ref, bd_ref, cd_ref):
    q = SSM_CHUNK
    nrow, p = c_re_ref.shape
    width = pw_re_ref.shape[1]
    gb = width // p
    row_g = lax.broadcasted_iota(jnp.int32, (nrow, width), 0) // (nrow // gb)
    col_g = lax.broadcasted_iota(jnp.int32, (nrow, width), 1) // p
    same_group = row_g == col_g

    def expand(ref):
        return jnp.where(same_group, jnp.tile(ref[...], (1, gb)), 0.0)

    def times_power(xr, xi, n):
        pr = pw_re_ref[n:n + 1, :]
        pi = pw_im_ref[n:n + 1, :]
        return xr * pr - xi * pi, xr * pi + xi * pr

    def dot_nt_f32(a, b):
        return lax.dot_general(a, b, (((1,), (1,)), ((), ())), precision=lax.Precision.HIGHEST,
                               preferred_element_type=F32)

    c_re, c_im = expand(c_re_ref), expand(c_im_ref)
    bt_re, bt_im = expand(bt_re_ref), expand(bt_im_ref)

    tm_ref[...] = jnp.zeros(tm_ref.shape, BF16)
    for tau in range(q):
        ca_re, ca_im = times_power(c_re, c_im, tau)
        blk = (dot_nt_f32(bt_re, ca_re) - dot_nt_f32(bt_im, ca_im)).astype(BF16)
        for jp in range(q - tau):
            tm_ref[jp * LANES:(jp + 1) * LANES, (jp + tau) * LANES:(jp + tau + 1) * LANES] = blk
    for jp in range(q):
        e_re, e_im = times_power(bt_re, bt_im, q - 1 - jp)
        bd_ref[jp * LANES:(jp + 1) * LANES, :width] = e_re.astype(BF16)
        bd_ref[jp * LANES:(jp + 1) * LANES, width:] = e_im.astype(BF16)
    for j in range(q):
        f_re, f_im = times_power(c_re, c_im, j + 1)
        cd_ref[:width, j * LANES:(j + 1) * LANES] = f_re.T.astype(BF16)
        cd_ref[width:, j * LANES:(j + 1) * LANES] = (-f_im).T.astype(BF16)


def _ssm_matrices(a_re, a_im, log_dt, b_re, b_im, c_re, c_im, nk):
    g, p = a_re.shape
    cch = b_re.shape[-1]
    q = SSM_CHUNK
    gb = LANES // cch
    nb = g // gb
    dt = jnp.exp(log_dt.astype(F32))[:, None]
    ar, ai = a_re.astype(F32), a_im.astype(F32)
    mag = jnp.exp(dt * ar)
    abar_re, abar_im = mag * jnp.cos(dt * ai), mag * jnp.sin(dt * ai)
    den = ar * ar + ai * ai
    nr, ni = abar_re - 1.0, abar_im
    coef_re = (nr * ar + ni * ai) / den
    coef_im = (ni * ar - nr * ai) / den
    br, bi = b_re.astype(F32), b_im.astype(F32)
    bbar_re = coef_re[..., None] * br - coef_im[..., None] * bi
    bbar_im = coef_re[..., None] * bi + coef_im[..., None] * br

    def power(n):
        n = jnp.asarray(n, F32)[:, None, None]
        m = jnp.exp(n * (dt * ar)[None])
        return m * jnp.cos(n * (dt * ai)[None]), m * jnp.sin(n * (dt * ai)[None])

    rows_c = lambda x: x.reshape(g * cch, p)
    c2 = [rows_c(c_re.astype(F32)), rows_c(c_im.astype(F32))]
    bt2 = [rows_c(bbar_re.transpose(0, 2, 1)), rows_c(bbar_im.transpose(0, 2, 1))]
    pw = [x.reshape(q + 1, g * p) for x in power(range(q + 1))]
    width = gb * p
    side = q * LANES
    par = pl.BlockSpec((LANES, p), lambda i: (i, 0))
    pws = pl.BlockSpec((q + 1, width), lambda i: (0, i))
    mat = lambda r, c: pl.BlockSpec((None, r, c), lambda i: (i, 0, 0))
    tmat, bd, cd = pl.pallas_call(
        _ssm_matrix_kernel,
        out_shape=(jax.ShapeDtypeStruct((nb, side, side), BF16),
                   jax.ShapeDtypeStruct((nb, side, 2 * width), BF16),
                   jax.ShapeDtypeStruct((nb, 2 * width, side), BF16)),
        grid=(nb,),
        in_specs=[par, par, par, par, pws, pws],
        out_specs=(mat(side, side), mat(side, 2 * width), mat(2 * width, side)),
        compiler_params=_cparams("arbitrary"),
        name="s5_matrices",
    )(c2[0], c2[1], bt2[0], bt2[1], pw[0], pw[1])

    def lanes(x):
        return x.reshape(x.shape[0], nb, width).transpose(1, 0, 2)

    aq = [lanes(x) for x in power([q])]
    aseg = [lanes(x) for x in power([q * nk])]
    return tmat, bd, cd, aq, aseg


def _ssm_kernel(u_ref, tm_ref, bd_ref, cd_ref, aqr_ref, aqi_ref, asr_ref, asi_ref,
                d_ref, y_ref, z_ref, yi_ref, sre_ref, sim_ref, xre_ref, xim_ref, xn_ref, *, bsz, nk):
    q = SSM_CHUNK
    rows = z_ref.shape[0]
    nslab = sre_ref.shape[0]
    half = nslab * LANES
    nseq = rows // nk
    nseg = nseq // bsz
    step_n = MXU_WIDTH

    for j in range(q):
        z_ref[:, j * LANES:(j + 1) * LANES] = u_ref[pl.ds(j, rows, stride=q), :].astype(BF16)
    z = z_ref[...]
    for c in range(0, q * LANES, step_n):
        yi_ref[:, c:c + step_n] = _dot(z, tm_ref[:, c:c + step_n])
    for c in range(0, 2 * half, step_n):
        s = _dot(z, bd_ref[:, c:c + step_n])
        dst = sre_ref if c < half else sim_ref
        for hh in range(step_n // LANES):
            slab = (c % half) // LANES + hh
            for sq in range(nseq):
                dst[slab, pl.ds(sq, nk, stride=nseq), :] = s[sq * nk:(sq + 1) * nk, hh * LANES:(hh + 1) * LANES]

    ar = [jnp.broadcast_to(aqr_ref[:, s * LANES:(s + 1) * LANES], (nseq, LANES)) for s in range(nslab)]
    ai = [jnp.broadcast_to(aqi_ref[:, s * LANES:(s + 1) * LANES], (nseq, LANES)) for s in range(nslab)]

    def step(k, carry):
        xr, xi = carry
        base = pl.multiple_of(k * nseq, nseq)
        nr, ni = [], []
        for s in range(nslab):
            xre_ref[s, pl.ds(base, nseq), :] = xr[s]
            xim_ref[s, pl.ds(base, nseq), :] = xi[s]
            nr.append(ar[s] * xr[s] - ai[s] * xi[s] + sre_ref[s, pl.ds(base, nseq), :])
            ni.append(ar[s] * xi[s] + ai[s] * xr[s] + sim_ref[s, pl.ds(base, nseq), :])
        return tuple(nr), tuple(ni)

    zero = tuple(jnp.zeros((nseq, LANES), F32) for _ in range(nslab))
    er, ei = lax.fori_loop(0, nk, step, (zero, zero), unroll=2)

    cr, ci = [], []
    for s in range(nslab):
        asr = asr_ref[:, s * LANES:(s + 1) * LANES]
        asi = asi_ref[:, s * LANES:(s + 1) * LANES]
        rows_r, rows_i = [], []
        for b in range(bsz):
            cre = jnp.zeros((1, LANES), F32)
            cim = jnp.zeros((1, LANES), F32)
            for sg in range(nseg):
                rows_r.append(cre)
                rows_i.append(cim)
                sq = b * nseg + sg
                cre, cim = (er[s][sq:sq + 1] + asr * cre - asi * cim,
                            ei[s][sq:sq + 1] + asr * cim + asi * cre)
        cr.append(jnp.concatenate(rows_r, axis=0))
        ci.append(jnp.concatenate(rows_i, axis=0))

    def fix(k, carry):
        wr, wi = carry
        base = pl.multiple_of(k * nseq, nseq)
        nr, ni = [], []
        for s in range(nslab):
            xre_ref[s, pl.ds(base, nseq), :] += wr[s]
            xim_ref[s, pl.ds(base, nseq), :] += wi[s]
            nr.append(ar[s] * wr[s] - ai[s] * wi[s])
            ni.append(ar[s] * wi[s] + ai[s] * wr[s])
        return tuple(nr), tuple(ni)

    lax.fori_loop(0, nk, fix, (tuple(cr), tuple(ci)), unroll=2)

    for s in range(nslab):
        for sq in range(nseq):
            rs = slice(sq * nk, (sq + 1) * nk)
            xn_ref[rs, s * LANES:(s + 1) * LANES] = xre_ref[s, pl.ds(sq, nk, stride=nseq), :].astype(BF16)
            xn_ref[rs, half + s * LANES:half + (s + 1) * LANES] = (
                xim_ref[s, pl.ds(sq, nk, stride=nseq), :].astype(BF16))
    xn = xn_ref[...]
    d = d_ref[...]
    for c in range(0, q * LANES, step_n):
        yy = yi_ref[:, c:c + step_n] + _dot(xn, cd_ref[:, c:c + step_n])
        for hh in range(step_n // LANES):
            j = c // LANES + hh
            y_ref[pl.ds(j, rows, stride=q), :] = (yy[:, hh * LANES:(hh + 1) * LANES]
                                                  + d * u_ref[pl.ds(j, rows, stride=q), :])


def _ssm(u2, a_re, a_im, log_dt, b_re, b_im, c_re, c_im, d_skip, bsz):
    t, sw = u2.shape
    q = SSM_CHUNK
    rows = t // q
    assert SUBLANES % bsz == 0 and rows % SUBLANES == 0
    nk = rows // SUBLANES
    tmat, bd, cd, aq, aseg = _ssm_matrices(a_re, a_im, log_dt, b_re, b_im, c_re, c_im, nk)
    nb, _, st = bd.shape
    half = st // 2
    nslab = half // LANES
    mat = lambda shape: pl.BlockSpec((None,) + shape, lambda i: (i, 0, 0))
    return pl.pallas_call(
        functools.partial(_ssm_kernel, bsz=bsz, nk=nk),
        out_shape=jax.ShapeDtypeStruct((t, sw), F32),
        grid=(nb,),
        in_specs=[pl.BlockSpec((t, LANES), lambda i: (0, i)),
                  mat((q * LANES, q * LANES)), mat((q * LANES, st)), mat((st, q * LANES)),
                  mat((1, half)), mat((1, half)), mat((1, half)), mat((1, half)),
                  pl.BlockSpec((1, LANES), lambda i: (0, i))],
        out_specs=pl.BlockSpec((t, LANES), lambda i: (0, i)),
        scratch_shapes=[pltpu.VMEM((rows, q * LANES), BF16), pltpu.VMEM((rows, q * LANES), F32)]
                       + [pltpu.VMEM((nslab, rows, LANES), F32) for _ in range(4)]
                       + [pltpu.VMEM((rows, st), BF16)],
        compiler_params=_cparams("arbitrary"),
        name="s5_scan",
    )(u2, tmat, bd, cd, aq[0], aq[1], aseg[0], aseg[1],
      d_skip.astype(F32).reshape(1, sw))


def _mix_kernel(x_ref, attn_ref, y_ref, wglu_ref, bglu_ref, wout_ref, gate_ref,
                g_ref, shift_ref, scale_ref, x1_ref, h2_ref):
    aw = attn_ref.shape[1]
    yg = jax.nn.gelu(y_ref[...])
    z = _dot(yg.astype(BF16), wglu_ref[...]) + bglu_ref[...]
    yy = yg * jax.nn.sigmoid(z)
    mix = _dot(attn_ref[...], wout_ref[:aw, :]) + _dot(yy.astype(BF16), wout_ref[aw:, :])
    x1 = x_ref[...] + gate_ref[...] * mix
    x1_ref[...] = x1
    h2_ref[...] = _norm_modulate(x1, g_ref[...], shift_ref[...], scale_ref[...]).astype(BF16)


def _mix(x2, attn, y, wglu_bf, bglu, wout_bf, mod4, norm_g, seq, tm):
    t, d = x2.shape
    aw = attn.shape[1]
    tiles_per_seq = seq // tm
    bidx = lambda i: i // tiles_per_seq
    modspec = lambda k: pl.BlockSpec((None, None, 1, d), lambda i: (bidx(i), k, 0, 0))
    return pl.pallas_call(
        _mix_kernel,
        out_shape=(jax.ShapeDtypeStruct((t, d), F32), jax.ShapeDtypeStruct((t, d), BF16)),
        grid=(t // tm,),
        in_specs=[pl.BlockSpec((tm, d), lambda i: (i, 0)),
                  pl.BlockSpec((tm, aw), lambda i: (i, 0)),
                  pl.BlockSpec((tm, aw), lambda i: (i, 0)),
                  _resident(wglu_bf.shape),
                  pl.BlockSpec((1, aw), lambda i: (0, 0)),
                  _resident(wout_bf.shape),
                  modspec(2),
                  pl.BlockSpec((1, d), lambda i: (0, 0)),
                  modspec(3), modspec(4)],
        out_specs=(pl.BlockSpec((tm, d), lambda i: (i, 0)), pl.BlockSpec((tm, d), lambda i: (i, 0))),
        compiler_params=_cparams("arbitrary"),
        name="mix_out_proj",
    )(x2, attn, y, wglu_bf, bglu.reshape(1, aw), wout_bf, mod4, norm_g.reshape(1, d), mod4, mod4)


def _ffn_kernel(h_ref, halo_ref, wa_ref, wg_ref, cw_ref, cb_ref, wd_ref, x1_ref, gate_ref,
                o_ref, a0_ref, a1_ref, g0_ref, g1_ref, act0_ref, act1_ref, acc_ref,
                *, tiles_per_seq, nf, units):
    n = pl.program_id(0)
    slot = n % 2
    f_down = jnp.maximum(n - 2, 0) % nf
    tm = h_ref.shape[0]

    @pl.when(n == 0)
    def _():
        a1_ref[...] = jnp.zeros(a1_ref.shape, F32)
        g1_ref[...] = jnp.zeros(g1_ref.shape, F32)
        act0_ref[...] = jnp.zeros(act0_ref.shape, BF16)

    @pl.when(f_down == 0)
    def _():
        acc_ref[...] = jnp.zeros(acc_ref.shape, F32)

    row_tile = jnp.minimum(n, units - 1) // nf

    def stages(a_rd, g_rd, a_wr, g_wr, act_rd, act_wr):
        halo = jnp.where(row_tile % tiles_per_seq == 0, jnp.zeros_like(halo_ref[...]), halo_ref[...])
        h = h_ref[...]
        h_ext = jnp.concatenate([halo, h], axis=0)
        act = act_rd[...]
        cw = cw_ref[...]
        cb = cb_ref[...]
        tf = a_wr.shape[1]
        d = acc_ref.shape[1]

        def down(c):
            cols = slice(c * (d // FFN_PARTS), (c + 1) * (d // FFN_PARTS))
            acc_ref[:, cols] += _dot(act, wd_ref[:, cols])

        def up(c, lhs, w_ref, dst):
            cols = slice(c * (2 * tf // FFN_PARTS), (c + 1) * (2 * tf // FFN_PARTS))
            dst[:, cols] = _dot(lhs, w_ref[:, cols])

        def conv_gelu(r):
            nrow = tm // (2 * FFN_PARTS)
            r0 = CONV_HALO + r * nrow
            conv = (cw[2:3] * a_rd[r0:r0 + nrow, :] + cw[1:2] * a_rd[r0 - 1:r0 - 1 + nrow, :]
                    + cw[0:1] * a_rd[r0 - 2:r0 - 2 + nrow, :] + cb)
            act_wr[r * nrow:(r + 1) * nrow, :] = (
                jax.nn.gelu(conv) * g_rd[r * nrow:(r + 1) * nrow, :]).astype(BF16)

        mxu_work = ([functools.partial(down, c) for c in range(2)]
                    + [functools.partial(up, 0, h_ext, wa_ref, a_wr), functools.partial(down, 2),
                       functools.partial(up, 0, h, wg_ref, g_wr), functools.partial(down, 3),
                       functools.partial(up, 1, h_ext, wa_ref, a_wr), functools.partial(up, 1, h, wg_ref, g_wr)])
        for r, matmul in enumerate(mxu_work):
            matmul()
            conv_gelu(r)

    @pl.when(slot == 0)
    def _():
        stages(a1_ref, g1_ref, a0_ref, g0_ref, act0_ref, act1_ref)

    @pl.when(slot == 1)
    def _():
        stages(a0_ref, g0_ref, a1_ref, g1_ref, act1_ref, act0_ref)

    @pl.when((f_down == nf - 1) & (n > 1))
    def _():
        o_ref[...] = x1_ref[...] + gate_ref[...] * acc_ref[...]


def _ffn(h2, x1, wup_bf, conv_w, conv_b, wdown_bf, mod4, seq, tm, tf):
    t, d = h2.shape
    dff = wdown_bf.shape[0]
    nf = dff // tf
    tiles_per_seq = seq // tm
    halo_blocks = tm // CONV_HALO
    units = (t // tm) * nf
    cur_tile = lambda n: jnp.minimum(n, units - 1) // nf
    cur_blk = lambda n: jnp.minimum(n, units - 1) % nf
    conv_blk = lambda n: jnp.clip(n - 1, 0, units - 1) % nf
    down_tile = lambda n: jnp.maximum(n - 2, 0) // nf
    down_blk = lambda n: jnp.maximum(n - 2, 0) % nf
    return pl.pallas_call(
        functools.partial(_ffn_kernel, tiles_per_seq=tiles_per_seq, nf=nf, units=units),
        out_shape=jax.ShapeDtypeStruct((t, d), F32),
        grid=(units + 2,),
        in_specs=[pl.BlockSpec((tm, d), lambda n: (cur_tile(n), 0)),
                  pl.BlockSpec((CONV_HALO, d), lambda n: (jnp.maximum(cur_tile(n) * halo_blocks - 1, 0), 0)),
                  pl.BlockSpec((d, tf), lambda n: (0, cur_blk(n))),
                  pl.BlockSpec((d, tf), lambda n: (0, nf + cur_blk(n))),
                  pl.BlockSpec((conv_w.shape[0], tf), lambda n: (0, conv_blk(n))),
                  pl.BlockSpec((1, tf), lambda n: (0, conv_blk(n))),
                  pl.BlockSpec((tf, d), lambda n: (down_blk(n), 0)),
                  pl.BlockSpec((tm, d), lambda n: (down_tile(n), 0)),
                  pl.BlockSpec((None, None, 1, d), lambda n: (down_tile(n) // tiles_per_seq, 5, 0, 0))],
        out_specs=pl.BlockSpec((tm, d), lambda n: (down_tile(n), 0)),
        scratch_shapes=[pltpu.VMEM((tm + CONV_HALO, tf), F32), pltpu.VMEM((tm + CONV_HALO, tf), F32),
                        pltpu.VMEM((tm, tf), F32), pltpu.VMEM((tm, tf), F32),
                        pltpu.VMEM((tm, tf), BF16), pltpu.VMEM((tm, tf), BF16),
                        pltpu.VMEM((tm, d), F32)],
        compiler_params=_cparams("arbitrary"),
        name="conv_ffn",
    )(h2, h2, wup_bf, wup_bf, conv_w, conv_b.reshape(1, dff), wdown_bf, x1, mod4)


def kernel(x, c, w_ada, b_ada, norm1_g, norm2_g, w_in, q_norm_g, k_norm_g, lambda_q1, lambda_k1,
           lambda_q2, lambda_k2, attn_sub_norm_g, ssm_a_re, ssm_a_im, ssm_log_dt, ssm_b_re, ssm_b_im,
           ssm_c_re, ssm_c_im, ssm_d, w_glu, b_glu, w_out, w_up, conv_w, conv_b, w_down):
    bsz, seq, d = x.shape
    depth = w_ada.shape[0]
    tm = min(ROW_TILE, seq)
    x2 = x.reshape(bsz * seq, d)
    for l in range(depth):
        lambda_init = 0.8 - 0.6 * math.exp(-0.3 * l)
        mod = _ada(c, w_ada[l], b_ada[l])
        mod4 = mod.reshape(bsz, 6, 1, d)
        qkv, u = _inproj(x2, mod4, norm1_g[l], w_in[l].astype(BF16), q_norm_g[l], k_norm_g[l], seq, tm)
        lamv = jnp.stack([lambda_q1[l], lambda_k1[l], lambda_q2[l], lambda_k2[l]]).astype(F32)
        attn = _attention(qkv, lamv, attn_sub_norm_g[l], bsz, seq, lambda_init, tm)
        y = _ssm(u, ssm_a_re[l], ssm_a_im[l], ssm_log_dt[l], ssm_b_re[l], ssm_b_im[l],
                 ssm_c_re[l], ssm_c_im[l], ssm_d[l], bsz)
        x1, h2 = _mix(x2, attn, y, w_glu[l].astype(BF16), b_glu[l], w_out[l].astype(BF16), mod4,
                      norm2_g[l], seq, tm)
        x2 = _ffn(h2, x1, w_up[l].astype(BF16), conv_w[l], conv_b[l], w_down[l].astype(BF16), mod4,
                  seq, tm, FFN_TF)
    return x2.reshape(bsz, seq, d)
```

```python
import functools
import math

import jax
import jax.numpy as jnp
from jax import lax
from jax.experimental import pallas as pl
from jax.experimental.pallas import tpu as pltpu

F32 = jnp.float32
BF16 = jnp.bfloat16

EPS = 1e-6
LANES = 128
SUBLANES = 8
MXU_WIDTH = 256
SSM_GROUP = 16
SSM_CHUNK = 8
CONV_HALO = 2 * SUBLANES
VMEM_LIMIT = 56 * 1024 * 1024
ROW_TILE = 512
FFN_TF = 512
FFN_PARTS = 4


def _cparams(*sem):
    return pltpu.CompilerParams(dimension_semantics=sem, vmem_limit_bytes=VMEM_LIMIT)


def _dot(a, b):
    return jnp.dot(a, b, preferred_element_type=F32)


def _dot_nt(a, b):
    return lax.dot_general(a, b, (((1,), (1,)), ((), ())), preferred_element_type=F32)


def _resident(shape):
    return pl.BlockSpec(shape, lambda *_: (0,) * len(shape), pipeline_mode=pl.Buffered(1))


def _ada_kernel(c_ref, w_ref, b_ref, o_ref):
    c = c_ref[...]
    o_ref[...] = _dot(c * jax.nn.sigmoid(c), w_ref[...]) + b_ref[...]


def _ada(c, w, b, tn=1024):
    bsz, d = c.shape
    n = w.shape[1]
    return pl.pallas_call(
        _ada_kernel,
        out_shape=jax.ShapeDtypeStruct((bsz, n), F32),
        grid=(n // tn,),
        in_specs=[pl.BlockSpec((bsz, d), lambda j: (0, 0)),
                  pl.BlockSpec((d, tn), lambda j: (0, j)),
                  pl.BlockSpec((1, tn), lambda j: (0, j))],
        out_specs=pl.BlockSpec((bsz, tn), lambda j: (0, j)),
        compiler_params=_cparams("arbitrary"),
        name="ada_ln",
    )(c, w, b.reshape(1, n))


def _norm_modulate(x, g, shift, scale):
    y = x * lax.rsqrt(jnp.mean(x * x, axis=-1, keepdims=True) + EPS)
    return y * g * (1.0 + scale) + shift


def _half_rms_norm(a, gain, post_scale):
    outs = []
    half = LANES // 2
    for cidx in range(a.shape[1] // LANES):
        blk = a[:, cidx * LANES:(cidx + 1) * LANES]
        sq = blk * blk
        lo = lax.broadcasted_iota(jnp.int32, blk.shape, 1) < half
        s_lo = jnp.sum(jnp.where(lo, sq, 0.0), axis=-1, keepdims=True)
        s_hi = jnp.sum(jnp.where(lo, 0.0, sq), axis=-1, keepdims=True)
        ms = jnp.where(lo, s_lo, s_hi) * (1.0 / half)
        outs.append(blk * lax.rsqrt(ms + EPS) * (gain * post_scale))
    return jnp.concatenate(outs, axis=-1)


def _inproj_kernel(x_ref, g_ref, shift_ref, scale_ref, w_ref, qg_ref, kg_ref,
                   qkv_ref, u_ref, *, q_scale):
    aw = u_ref.shape[1]
    h = _norm_modulate(x_ref[...], g_ref[...], shift_ref[...], scale_ref[...]).astype(BF16)
    qkv_ref[:, :aw] = _half_rms_norm(_dot(h, w_ref[:, :aw]), qg_ref[...], q_scale).astype(BF16)
    qkv_ref[:, aw:2 * aw] = _half_rms_norm(_dot(h, w_ref[:, aw:2 * aw]), kg_ref[...], 1.0).astype(BF16)
    qkv_ref[:, 2 * aw:] = _dot(h, w_ref[:, 2 * aw:3 * aw]).astype(BF16)
    u_ref[...] = _dot(h, w_ref[:, 3 * aw:])


def _inproj(x2, mod4, norm_g, w_bf, qg, kg, seq, tm):
    t, d = x2.shape
    aw = w_bf.shape[1] // 4
    tiles_per_seq = seq // tm
    bidx = lambda i: i // tiles_per_seq
    qg2 = jnp.tile(qg, 2).reshape(1, LANES)
    kg2 = jnp.tile(kg, 2).reshape(1, LANES)
    head_dim = qg.shape[0]
    return pl.pallas_call(
        functools.partial(_inproj_kernel, q_scale=head_dim ** -0.5 * math.log2(math.e)),
        out_shape=(jax.ShapeDtypeStruct((t, 3 * aw), BF16), jax.ShapeDtypeStruct((t, aw), F32)),
        grid=(t // tm,),
        in_specs=[pl.BlockSpec((tm, d), lambda i: (i, 0)),
                  pl.BlockSpec((1, d), lambda i: (0, 0)),
                  pl.BlockSpec((None, None, 1, d), lambda i: (bidx(i), 0, 0, 0)),
                  pl.BlockSpec((None, None, 1, d), lambda i: (bidx(i), 1, 0, 0)),
                  _resident(w_bf.shape),
                  pl.BlockSpec((1, LANES), lambda i: (0, 0)),
                  pl.BlockSpec((1, LANES), lambda i: (0, 0))],
        out_specs=(pl.BlockSpec((tm, 3 * aw), lambda i: (i, 0)),
                   pl.BlockSpec((tm, aw), lambda i: (i, 0))),
        compiler_params=_cparams("arbitrary"),
        name="in_proj",
    )(x2, norm_g.reshape(1, d), mod4, mod4, w_bf, qg2, kg2)


ATTN_TQ = 512
ATTN_TK = 512
ATTN_ROWS = 64
ATTN_HEADS_PER_STEP = 1
NEG_BIG = -1e30
POS_SPLIT = 64


def _attn_kernel(lamv_ref, gsub_ref, coef_ref, pos_ref, q_ref, k_ref, v_ref, o_ref,
                 kaug_ref, vaug_ref, qaug_ref, s_ref, p_ref, m_ref, alpha_ref, acc_ref,
                 *, tq, lambda_init):
    i = pl.program_id(2)
    half = LANES // 2
    heads = range(ATTN_HEADS_PER_STEP)
    head_lanes = [slice(hd * LANES, (hd + 1) * LANES) for hd in heads]

    @pl.when(i == 0)
    def _():
        for hd in heads:
            kaug_ref[hd, :, :LANES] = k_ref[:, head_lanes[hd]]
            kaug_ref[hd, :, LANES:] = pos_ref[...]
            vaug_ref[hd, :, :LANES] = v_ref[:, head_lanes[hd]]
            vaug_ref[hd, :, LANES:] = jnp.ones((v_ref.shape[0], LANES), BF16)

    for hd in heads:
        q = q_ref[:, head_lanes[hd]]
        lane = lax.broadcasted_iota(jnp.int32, q.shape, 1)
        zero = jnp.zeros_like(q)
        qaug_ref[hd, 0, :, :LANES] = jnp.where(lane < half, q, zero)
        qaug_ref[hd, 1, :, :LANES] = jnp.where(lane >= half, q, zero)
        coef = jnp.broadcast_to(coef_ref[hd], q.shape)
        qaug_ref[hd, 0, :, LANES:] = coef
        qaug_ref[hd, 1, :, LANES:] = coef

    m_ref[...] = jnp.full(m_ref.shape, NEG_BIG, F32)
    acc_ref[...] = jnp.zeros(acc_ref.shape, F32)
    alpha_ref[:, 1] = jnp.ones((len(heads),) + alpha_ref.shape[2:], F32)
    p_ref[:, 1] = jnp.zeros((len(heads),) + p_ref.shape[2:], BF16)

    nsub = tq // ATTN_ROWS
    tk = s_ref.shape[-1]
    hk = tk // 2
    per_tile = tq // tk

    def scores(j, idx, part, hd):
        start = pl.multiple_of(j * tk + part * hk, hk)
        s_ref[hd, idx, :, part * hk:(part + 1) * hk] = _dot_nt(qaug_ref[hd, idx],
                                                               kaug_ref[hd, pl.ds(start, hk), :])

    def softmax(idx, mask_off, chunks, hd):
        for r in chunks:
            rows = slice(r * ATTN_ROWS, (r + 1) * ATTN_ROWS)
            s = s_ref[hd, idx, rows, :]
            if mask_off is not None:
                row = lax.broadcasted_iota(jnp.int32, s.shape, 0) + r * ATTN_ROWS
                col = lax.broadcasted_iota(jnp.int32, s.shape, 1) + mask_off
                s = jnp.where(col <= row, s, NEG_BIG)
            m_old = m_ref[hd, idx, rows, :]
            m_new = jnp.maximum(m_old, jnp.max(s, axis=-1, keepdims=True))
            alpha_ref[hd, idx, rows, :] = jnp.exp2(m_old - m_new)
            m_ref[hd, idx, rows, :] = m_new
            p_ref[hd, idx, rows, :] = jnp.exp2(s - jnp.tile(m_new, (1, tk // LANES))).astype(BF16)

    def values(j, idx, hd):
        start = pl.multiple_of(jnp.maximum(j, 0) * tk, tk)
        pv = _dot(p_ref[hd, idx], vaug_ref[hd, pl.ds(start, tk), :])
        acc_ref[hd, idx] = jnp.tile(alpha_ref[hd, idx], (1, 2)) * acc_ref[hd, idx] + pv

    def unit(j_s, idx_s, idx_sm, mask_off, j_v, idx_v):
        matmuls = []
        if j_s is not None:
            matmuls += [functools.partial(scores, j_s, idx_s, part) for part in range(2)]
        matmuls.append(functools.partial(values, j_v, idx_v))
        bounds = [round(k * nsub / len(matmuls)) for k in range(len(matmuls) + 1)]
        for k, matmul in enumerate(matmuls):
            for hd in heads:
                matmul(hd)
            for hd in heads:
                softmax(idx_sm, mask_off, range(bounds[k], bounds[k + 1]), hd)

    for part in range(2):
        for hd in heads:
            scores(0, 0, part, hd)

    def block(j):
        unit(j, 1, 0, None, j - 1, 1)
        unit(j + 1, 0, 1, None, j, 0)

    def two_blocks(jj, carry):
        block(2 * jj)
        block(2 * jj + 1)
        return carry

    nfull = i * per_tile
    lax.fori_loop(0, nfull // 2, two_blocks, 0)

    if per_tile % 2 == 1:
        @pl.when(nfull % 2 == 1)
        def _():
            block(nfull - 1)

    tail = [(nfull + t, idx, t * tk) for t in range(per_tile) for idx in range(2)]
    j_v, idx_v = nfull - 1, 1
    for n, (j_sm, idx_sm, off) in enumerate(tail):
        j_s, idx_s = tail[n + 1][:2] if n + 1 < len(tail) else (None, None)
        unit(j_s, idx_s, idx_sm, off, j_v, idx_v)
        j_v, idx_v = j_sm, idx_sm
    for hd in heads:
        values(j_v, idx_v, hd)

    lv = lamv_ref[...]
    lam = (jnp.exp(jnp.sum(lv[0:1] * lv[1:2], axis=-1, keepdims=True))
           - jnp.exp(jnp.sum(lv[2:3] * lv[3:4], axis=-1, keepdims=True)) + lambda_init)
    for hd in heads:
        a0 = acc_ref[hd, 0]
        a1 = acc_ref[hd, 1]
        o = a0[:, :LANES] / a0[:, LANES:] - lam * (a1[:, :LANES] / a1[:, LANES:])
        o = o * lax.rsqrt(jnp.mean(o * o, axis=-1, keepdims=True) + EPS)
        o_ref[:, head_lanes[hd]] = (o * gsub_ref[...] * (1.0 - lambda_init)).astype(BF16)


def _split_bf16(x, n):
    parts = []
    for _ in range(n):
        p = x.astype(BF16)
        parts.append(p)
        x = x - p.astype(F32)
    return parts


def _attention(qkv, lamv, gsub, bsz, seq, lambda_init, tq, tk):
    t = qkv.shape[0]
    aw = qkv.shape[1] // 3
    heads = aw // LANES
    hps = ATTN_HEADS_PER_STEP
    groups = heads // hps
    nq = seq // tq
    c = jnp.asarray([2.0 ** (-8.0 * (i + 1) / heads) * math.log2(math.e) for i in range(heads)], F32)
    cparts = _split_bf16(c, 3)
    coef = jnp.stack([p * POS_SPLIT for p in cparts] + cparts, axis=-1)
    coef = jnp.pad(coef, ((0, 0), (0, LANES - coef.shape[1]))).reshape(heads, 1, LANES)
    kpos = jnp.arange(seq, dtype=jnp.int32)
    hi = (kpos // POS_SPLIT).astype(BF16)
    lo = (kpos % POS_SPLIT).astype(BF16)
    pos = jnp.pad(jnp.stack([hi, hi, hi, lo, lo, lo], axis=-1), ((0, 0), (0, LANES - 6)))
    return pl.pallas_call(
        functools.partial(_attn_kernel, tq=tq, lambda_init=lambda_init),
        out_shape=jax.ShapeDtypeStruct((t, aw), BF16),
        grid=(bsz, groups, nq),
        in_specs=[pl.BlockSpec((4, LANES // 2), lambda b, h, i: (0, 0)),
                  pl.BlockSpec((1, LANES), lambda b, h, i: (0, 0)),
                  pl.BlockSpec((hps, 1, LANES), lambda b, h, i: (h, 0, 0)),
                  pl.BlockSpec((seq, LANES), lambda b, h, i: (0, 0)),
                  pl.BlockSpec((tq, hps * LANES), lambda b, h, i: (b * nq + i, h)),
                  pl.BlockSpec((seq, hps * LANES), lambda b, h, i: (b, groups + h)),
                  pl.BlockSpec((seq, hps * LANES), lambda b, h, i: (b, 2 * groups + h))],
        out_specs=pl.BlockSpec((tq, hps * LANES), lambda b, h, i: (b * nq + i, h)),
        scratch_shapes=[pltpu.VMEM((hps, seq, 2 * LANES), BF16), pltpu.VMEM((hps, seq, 2 * LANES), BF16),
                        pltpu.VMEM((hps, 2, tq, 2 * LANES), BF16),
                        pltpu.VMEM((hps, 2, tq, tk), F32), pltpu.VMEM((hps, 2, tq, tk), BF16),
                        pltpu.VMEM((hps, 2, tq, LANES), F32), pltpu.VMEM((hps, 2, tq, LANES), F32),
                        pltpu.VMEM((hps, 2, tq, 2 * LANES), F32)],
        compiler_params=_cparams("arbitrary", "arbitrary", "arbitrary"),
        name="diff_attn",
    )(lamv, gsub.reshape(1, LANES), coef, pos, qkv, qkv, qkv)


def _ssm_matrix_kernel(c_re_ref, c_im_ref, bt_re_ref, bt_im_ref, pw_re_ref, pw_im_ref,
                       tm_ref, bd_ref, cd_ref):
    q = SSM_CHUNK
    nrow, p = c_re_ref.shape
    width = pw_re_ref.shape[1]
    gb = width // p
    row_g = lax.broadcasted_iota(jnp.int32, (nrow, width), 0) // (nrow // gb)
    col_g = lax.broadcasted_iota(jnp.int32, (nrow, width), 1) // p
    same_group = row_g == col_g

    def expand(ref):
        return jnp.where(same_group, jnp.tile(ref[...], (1, gb)), 0.0)

    def times_power(xr, xi, n):
        pr = pw_re_ref[n:n + 1, :]
        pi = pw_im_ref[n:n + 1, :]
        return xr * pr - xi * pi, xr * pi + xi * pr

    def dot_nt_f32(a, b):
        a_hi, b_hi = a.astype(BF16), b.astype(BF16)
        a_lo = (a - a_hi.astype(F32)).astype(BF16)
        b_lo = (b - b_hi.astype(F32)).astype(BF16)
        return _dot_nt(a_hi, b_hi) + _dot_nt(a_hi, b_lo) + _dot_nt(a_lo, b_hi)

    c_re, c_im = expand(c_re_ref), expand(c_im_ref)
    bt_re, bt_im = expand(bt_re_ref), expand(bt_im_ref)

    tm_ref[...] = jnp.zeros(tm_ref.shape, BF16)
    for tau in range(q):
        ca_re, ca_im = times_power(c_re, c_im, tau)
        blk = (dot_nt_f32(bt_re, ca_re) - dot_nt_f32(bt_im, ca_im)).astype(BF16)
        for jp in range(q - tau):
            tm_ref[jp * LANES:(jp + 1) * LANES, (jp + tau) * LANES:(jp + tau + 1) * LANES] = blk
    for jp in range(q):
        e_re, e_im = times_power(bt_re, bt_im, q - 1 - jp)
        bd_ref[jp * LANES:(jp + 1) * LANES, :width] = e_re.astype(BF16)
        bd_ref[jp * LANES:(jp + 1) * LANES, width:] = e_im.astype(BF16)
    for j in range(q):
        f_re, f_im = times_power(c_re, c_im, j + 1)
        cd_ref[:width, j * LANES:(j + 1) * LANES] = f_re.T.astype(BF16)
        cd_ref[width:, j * LANES:(j + 1) * LANES] = (-f_im).T.astype(BF16)


def _ssm_matrices(a_re, a_im, log_dt, b_re, b_im, c_re, c_im, nk):
    g, p = a_re.shape
    cch = b_re.shape[-1]
    q = SSM_CHUNK
    gb = LANES // cch
    nb = g // gb
    dt = jnp.exp(log_dt.astype(F32))[:, None]
    ar, ai = a_re.astype(F32), a_im.astype(F32)
    mag = jnp.exp(dt * ar)
    abar_re, abar_im = mag * jnp.cos(dt * ai), mag * jnp.sin(dt * ai)
    den = ar * ar + ai * ai
    nr, ni = abar_re - 1.0, abar_im
    coef_re = (nr * ar + ni * ai) / den
    coef_im = (ni * ar - nr * ai) / den
    br, bi = b_re.astype(F32), b_im.astype(F32)
    bbar_re = coef_re[..., None] * br - coef_im[..., None] * bi
    bbar_im = coef_re[..., None] * bi + coef_im[..., None] * br

    def power(n):
        n = jnp.asarray(n, F32)[:, None, None]
        m = jnp.exp(n * (dt * ar)[None])
        return m * jnp.cos(n * (dt * ai)[None]), m * jnp.sin(n * (dt * ai)[None])

    rows_c = lambda x: x.reshape(g * cch, p)
    c2 = [rows_c(c_re.astype(F32)), rows_c(c_im.astype(F32))]
    bt2 = [rows_c(bbar_re.transpose(0, 2, 1)), rows_c(bbar_im.transpose(0, 2, 1))]
    pw = [x.reshape(q + 1, g * p) for x in power(range(q + 1))]
    width = gb * p
    side = q * LANES
    par = pl.BlockSpec((LANES, p), lambda i: (i, 0))
    pws = pl.BlockSpec((q + 1, width), lambda i: (0, i))
    mat = lambda r, c: pl.BlockSpec((None, r, c), lambda i: (i, 0, 0))
    tmat, bd, cd = pl.pallas_call(
        _ssm_matrix_kernel,
        out_shape=(jax.ShapeDtypeStruct((nb, side, side), BF16),
                   jax.ShapeDtypeStruct((nb, side, 2 * width), BF16),
                   jax.ShapeDtypeStruct((nb, 2 * width, side), BF16)),
        grid=(nb,),
        in_specs=[par, par, par, par, pws, pws],
        out_specs=(mat(side, side), mat(side, 2 * width), mat(2 * width, side)),
        compiler_params=_cparams("arbitrary"),
        name="s5_matrices",
    )(c2[0], c2[1], bt2[0], bt2[1], pw[0], pw[1])

    def lanes(x):
        return x.reshape(x.shape[0], nb, width).transpose(1, 0, 2)

    aq = [lanes(x) for x in power([q])]
    aseg = [lanes(x) for x in power([q * nk])]
    return tmat, bd, cd, aq, aseg


def _ssm_kernel(u_ref, tm_ref, bd_ref, cd_ref, aqr_ref, aqi_ref, asr_ref, asi_ref,
                d_ref, y_ref, z_ref, yi_ref, sre_ref, sim_ref, xre_ref, xim_ref, xn_ref, *, bsz, nk):
    q = SSM_CHUNK
    rows = z_ref.shape[0]
    nslab = sre_ref.shape[0]
    half = nslab * LANES
    nseq = rows // nk
    nseg = nseq // bsz
    step_n = MXU_WIDTH

    for j in range(q):
        z_ref[:, j * LANES:(j + 1) * LANES] = u_ref[pl.ds(j, rows, stride=q), :].astype(BF16)
    z = z_ref[...]
    for c in range(0, q * LANES, step_n):
        yi_ref[:, c:c + step_n] = _dot(z, tm_ref[:, c:c + step_n])
    for c in range(0, 2 * half, step_n):
        s = _dot(z, bd_ref[:, c:c + step_n])
        dst = sre_ref if c < half else sim_ref
        for hh in range(step_n // LANES):
            slab = (c % half) // LANES + hh
            for sq in range(nseq):
                dst[slab, pl.ds(sq, nk, stride=nseq), :] = s[sq * nk:(sq + 1) * nk, hh * LANES:(hh + 1) * LANES]

    ar = [jnp.broadcast_to(aqr_ref[:, s * LANES:(s + 1) * LANES], (nseq, LANES)) for s in range(nslab)]
    ai = [jnp.broadcast_to(aqi_ref[:, s * LANES:(s + 1) * LANES], (nseq, LANES)) for s in range(nslab)]

    def step(k, carry):
        xr, xi = carry
        base = pl.multiple_of(k * nseq, nseq)
        nr, ni = [], []
        for s in range(nslab):
            xre_ref[s, pl.ds(base, nseq), :] = xr[s]
            xim_ref[s, pl.ds(base, nseq), :] = xi[s]
            nr.append(ar[s] * xr[s] - ai[s] * xi[s] + sre_ref[s, pl.ds(base, nseq), :])
            ni.append(ar[s] * xi[s] + ai[s] * xr[s] + sim_ref[s, pl.ds(base, nseq), :])
        return tuple(nr), tuple(ni)

    zero = tuple(jnp.zeros((nseq, LANES), F32) for _ in range(nslab))
    er, ei = lax.fori_loop(0, nk, step, (zero, zero), unroll=2)

    cr, ci = [], []
    for s in range(nslab):
        asr = asr_ref[:, s * LANES:(s + 1) * LANES]
        asi = asi_ref[:, s * LANES:(s + 1) * LANES]
        rows_r, rows_i = [], []
        for b in range(bsz):
            cre = jnp.zeros((1, LANES), F32)
            cim = jnp.zeros((1, LANES), F32)
            for sg in range(nseg):
                rows_r.append(cre)
                rows_i.append(cim)
                sq = b * nseg + sg
                cre, cim = (er[s][sq:sq + 1] + asr * cre - asi * cim,
                            ei[s][sq:sq + 1] + asr * cim + asi * cre)
        cr.append(jnp.concatenate(rows_r, axis=0))
        ci.append(jnp.concatenate(rows_i, axis=0))

    def fix(k, carry):
        wr, wi = carry
        base = pl.multiple_of(k * nseq, nseq)
        nr, ni = [], []
        for s in range(nslab):
            xre_ref[s, pl.ds(base, nseq), :] += wr[s]
            xim_ref[s, pl.ds(base, nseq), :] += wi[s]
            nr.append(ar[s] * wr[s] - ai[s] * wi[s])
            ni.append(ar[s] * wi[s] + ai[s] * wr[s])
        return tuple(nr), tuple(ni)

    lax.fori_loop(0, nk, fix, (tuple(cr), tuple(ci)), unroll=2)

    for s in range(nslab):
        for sq in range(nseq):
            rs = slice(sq * nk, (sq + 1) * nk)
            xn_ref[rs, s * LANES:(s + 1) * LANES] = xre_ref[s, pl.ds(sq, nk, stride=nseq), :].astype(BF16)
            xn_ref[rs, half + s * LANES:half + (s + 1) * LANES] = (
                xim_ref[s, pl.ds(sq, nk, stride=nseq), :].astype(BF16))
    xn = xn_ref[...]
    d = d_ref[...]
    for c in range(0, q * LANES, step_n):
        yy = yi_ref[:, c:c + step_n] + _dot(xn, cd_ref[:, c:c + step_n])
        for hh in range(step_n // LANES):
            j = c // LANES + hh
            y_ref[pl.ds(j, rows, stride=q), :] = (yy[:, hh * LANES:(hh + 1) * LANES]
                                                  + d * u_ref[pl.ds(j, rows, stride=q), :])


def _ssm(u2, a_re, a_im, log_dt, b_re, b_im, c_re, c_im, d_skip, bsz):
    t, sw = u2.shape
    q = SSM_CHUNK
    rows = t // q
    assert SUBLANES % bsz == 0 and rows % SUBLANES == 0
    nk = rows // SUBLANES
    tmat, bd, cd, aq, aseg = _ssm_matrices(a_re, a_im, log_dt, b_re, b_im, c_re, c_im, nk)
    nb, _, st = bd.shape
    half = st // 2
    nslab = half // LANES
    mat = lambda shape: pl.BlockSpec((None,) + shape, lambda i: (i, 0, 0))
    return pl.pallas_call(
        functools.partial(_ssm_kernel, bsz=bsz, nk=nk),
        out_shape=jax.ShapeDtypeStruct((t, sw), F32),
        grid=(nb,),
        in_specs=[pl.BlockSpec((t, LANES), lambda i: (0, i)),
                  mat((q * LANES, q * LANES)), mat((q * LANES, st)), mat((st, q * LANES)),
                  mat((1, half)), mat((1, half)), mat((1, half)), mat((1, half)),
                  pl.BlockSpec((1, LANES), lambda i: (0, i))],
        out_specs=pl.BlockSpec((t, LANES), lambda i: (0, i)),
        scratch_shapes=[pltpu.VMEM((rows, q * LANES), BF16), pltpu.VMEM((rows, q * LANES), F32)]
                       + [pltpu.VMEM((nslab, rows, LANES), F32) for _ in range(4)]
                       + [pltpu.VMEM((rows, st), BF16)],
        compiler_params=_cparams("arbitrary"),
        name="s5_scan",
    )(u2, tmat, bd, cd, aq[0], aq[1], aseg[0], aseg[1],
      d_skip.astype(F32).reshape(1, sw))


def _mix_kernel(x_ref, attn_ref, y_ref, wglu_ref, bglu_ref, wout_ref, gate_ref,
                g_ref, shift_ref, scale_ref, x1_ref, h2_ref):
    aw = attn_ref.shape[1]
    yg = jax.nn.gelu(y_ref[...])
    z = _dot(yg.astype(BF16), wglu_ref[...]) + bglu_ref[...]
    yy = yg * jax.nn.sigmoid(z)
    mix = _dot(attn_ref[...], wout_ref[:aw, :]) + _dot(yy.astype(BF16), wout_ref[aw:, :])
    x1 = x_ref[...] + gate_ref[...] * mix
    x1_ref[...] = x1
    h2_ref[...] = _norm_modulate(x1, g_ref[...], shift_ref[...], scale_ref[...]).astype(BF16)


def _mix(x2, attn, y, wglu_bf, bglu, wout_bf, mod4, norm_g, seq, tm):
    t, d = x2.shape
    aw = attn.shape[1]
    tiles_per_seq = seq // tm
    bidx = lambda i: i // tiles_per_seq
    modspec = lambda k: pl.BlockSpec((None, None, 1, d), lambda i: (bidx(i), k, 0, 0))
    return pl.pallas_call(
        _mix_kernel,
        out_shape=(jax.ShapeDtypeStruct((t, d), F32), jax.ShapeDtypeStruct((t, d), BF16)),
        grid=(t // tm,),
        in_specs=[pl.BlockSpec((tm, d), lambda i: (i, 0)),
                  pl.BlockSpec((tm, aw), lambda i: (i, 0)),
                  pl.BlockSpec((tm, aw), lambda i: (i, 0)),
                  _resident(wglu_bf.shape),
                  pl.BlockSpec((1, aw), lambda i: (0, 0)),
                  _resident(wout_bf.shape),
                  modspec(2),
                  pl.BlockSpec((1, d), lambda i: (0, 0)),
                  modspec(3), modspec(4)],
        out_specs=(pl.BlockSpec((tm, d), lambda i: (i, 0)), pl.BlockSpec((tm, d), lambda i: (i, 0))),
        compiler_params=_cparams("arbitrary"),
        name="mix_out_proj",
    )(x2, attn, y, wglu_bf, bglu.reshape(1, aw), wout_bf, mod4, norm_g.reshape(1, d), mod4, mod4)


def _ffn_kernel(h_ref, halo_ref, wa_ref, wg_ref, cw_ref, cb_ref, wd_ref, x1_ref, gate_ref,
                o_ref, a0_ref, a1_ref, g0_ref, g1_ref, act0_ref, act1_ref, acc_ref,
                *, tiles_per_seq, nf, units):
    n = pl.program_id(0)
    slot = n % 2
    f_down = jnp.maximum(n - 2, 0) % nf
    tm = h_ref.shape[0]

    @pl.when(n == 0)
    def _():
        a1_ref[...] = jnp.zeros(a1_ref.shape, F32)
        g1_ref[...] = jnp.zeros(g1_ref.shape, F32)
        act0_ref[...] = jnp.zeros(act0_ref.shape, BF16)

    @pl.when(f_down == 0)
    def _():
        acc_ref[...] = jnp.zeros(acc_ref.shape, F32)

    row_tile = jnp.minimum(n, units - 1) // nf

    def stages(a_rd, g_rd, a_wr, g_wr, act_rd, act_wr):
        halo = jnp.where(row_tile % tiles_per_seq == 0, jnp.zeros_like(halo_ref[...]), halo_ref[...])
        h = h_ref[...]
        h_ext = jnp.concatenate([halo, h], axis=0)
        act = act_rd[...]
        cw = cw_ref[...]
        cb = cb_ref[...]
        tf = a_wr.shape[1]
        d = acc_ref.shape[1]

        def down(c):
            cols = slice(c * (d // FFN_PARTS), (c + 1) * (d // FFN_PARTS))
            acc_ref[:, cols] += _dot(act, wd_ref[:, cols])

        def up(c, lhs, w_ref, dst):
            cols = slice(c * (2 * tf // FFN_PARTS), (c + 1) * (2 * tf // FFN_PARTS))
            dst[:, cols] = _dot(lhs, w_ref[:, cols])

        def conv_gelu(r):
            nrow = tm // (2 * FFN_PARTS)
            r0 = CONV_HALO + r * nrow
            conv = (cw[2:3] * a_rd[r0:r0 + nrow, :] + cw[1:2] * a_rd[r0 - 1:r0 - 1 + nrow, :]
                    + cw[0:1] * a_rd[r0 - 2:r0 - 2 + nrow, :] + cb)
            act_wr[r * nrow:(r + 1) * nrow, :] = (
                jax.nn.gelu(conv) * g_rd[r * nrow:(r + 1) * nrow, :]).astype(BF16)

        mxu_work = ([functools.partial(down, c) for c in range(2)]
                    + [functools.partial(up, 0, h_ext, wa_ref, a_wr), functools.partial(down, 2),
                       functools.partial(up, 0, h, wg_ref, g_wr), functools.partial(down, 3),
                       functools.partial(up, 1, h_ext, wa_ref, a_wr), functools.partial(up, 1, h, wg_ref, g_wr)])
        for r, matmul in enumerate(mxu_work):
            matmul()
            conv_gelu(r)

    @pl.when(slot == 0)
    def _():
        stages(a1_ref, g1_ref, a0_ref, g0_ref, act0_ref, act1_ref)

    @pl.when(slot == 1)
    def _():
        stages(a0_ref, g0_ref, a1_ref, g1_ref, act1_ref, act0_ref)

    @pl.when((f_down == nf - 1) & (n > 1))
    def _():
        o_ref[...] = x1_ref[...] + gate_ref[...] * acc_ref[...]


def _ffn(h2, x1, wup_bf, conv_w, conv_b, wdown_bf, mod4, seq, tm, tf):
    t, d = h2.shape
    dff = wdown_bf.shape[0]
    nf = dff // tf
    tiles_per_seq = seq // tm
    halo_blocks = tm // CONV_HALO
    units = (t // tm) * nf
    cur_tile = lambda n: jnp.minimum(n, units - 1) // nf
    cur_blk = lambda n: jnp.minimum(n, units - 1) % nf
    conv_blk = lambda n: jnp.clip(n - 1, 0, units - 1) % nf
    down_tile = lambda n: jnp.maximum(n - 2, 0) // nf
    down_blk = lambda n: jnp.maximum(n - 2, 0) % nf
    return pl.pallas_call(
        functools.partial(_ffn_kernel, tiles_per_seq=tiles_per_seq, nf=nf, units=units),
        out_shape=jax.ShapeDtypeStruct((t, d), F32),
        grid=(units + 2,),
        in_specs=[pl.BlockSpec((tm, d), lambda n: (cur_tile(n), 0)),
                  pl.BlockSpec((CONV_HALO, d), lambda n: (jnp.maximum(cur_tile(n) * halo_blocks - 1, 0), 0)),
                  pl.BlockSpec((d, tf), lambda n: (0, cur_blk(n))),
                  pl.BlockSpec((d, tf), lambda n: (0, nf + cur_blk(n))),
                  pl.BlockSpec((conv_w.shape[0], tf), lambda n: (0, conv_blk(n))),
                  pl.BlockSpec((1, tf), lambda n: (0, conv_blk(n))),
                  pl.BlockSpec((tf, d), lambda n: (down_blk(n), 0)),
                  pl.BlockSpec((tm, d), lambda n: (down_tile(n), 0)),
                  pl.BlockSpec((None, None, 1, d), lambda n: (down_tile(n) // tiles_per_seq, 5, 0, 0))],
        out_specs=pl.BlockSpec((tm, d), lambda n: (down_tile(n), 0)),
        scratch_shapes=[pltpu.VMEM((tm + CONV_HALO, tf), F32), pltpu.VMEM((tm + CONV_HALO, tf), F32),
                        pltpu.VMEM((tm, tf), F32), pltpu.VMEM((tm, tf), F32),
                        pltpu.VMEM((tm, tf), BF16), pltpu.VMEM((tm, tf), BF16),
                        pltpu.VMEM((tm, d), F32)],
        compiler_params=_cparams("arbitrary"),
        name="conv_ffn",
    )(h2, h2, wup_bf, wup_bf, conv_w, conv_b.reshape(1, dff), wdown_bf, x1, mod4)


def kernel(x, c, w_ada, b_ada, norm1_g, norm2_g, w_in, q_norm_g, k_norm_g, lambda_q1, lambda_k1,
           lambda_q2, lambda_k2, attn_sub_norm_g, ssm_a_re, ssm_a_im, ssm_log_dt, ssm_b_re, ssm_b_im,
           ssm_c_re, ssm_c_im, ssm_d, w_glu, b_glu, w_out, w_up, conv_w, conv_b, w_down):
    bsz, seq, d = x.shape
    depth = w_ada.shape[0]
    tm = min(ROW_TILE, seq)
    x2 = x.reshape(bsz * seq, d)
    for l in range(depth):
        lambda_init = 0.8 - 0.6 * math.exp(-0.3 * l)
        mod = _ada(c, w_ada[l], b_ada[l])
        mod4 = mod.reshape(bsz, 6, 1, d)
        qkv, u = _inproj(x2, mod4, norm1_g[l], w_in[l].astype(BF16), q_norm_g[l], k_norm_g[l], seq, tm)
        lamv = jnp.stack([lambda_q1[l], lambda_k1[l], lambda_q2[l], lambda_k2[l]]).astype(F32)
        attn = _attention(qkv, lamv, attn_sub_norm_g[l], bsz, seq, lambda_init,
                          min(ATTN_TQ, seq), min(ATTN_TK, seq))
        y = _ssm(u, ssm_a_re[l], ssm_a_im[l], ssm_log_dt[l], ssm_b_re[l], ssm_b_im[l],
                 ssm_c_re[l], ssm_c_im[l], ssm_d[l], bsz)
        x1, h2 = _mix(x2, attn, y, w_glu[l].astype(BF16), b_glu[l], w_out[l].astype(BF16), mod4,
                      norm2_g[l], seq, tm)
        x2 = _ffn(h2, x1, w_up[l].astype(BF16), conv_w[l], conv_b[l], w_down[l].astype(BF16), mod4,
                  seq, tm, FFN_TF)
    return x2.reshape(bsz, seq, d)
```

```python
import functools
import math

import jax
import jax.numpy as jnp
from jax import lax
from jax.experimental import pallas as pl
from jax.experimental.pallas import tpu as pltpu

F32 = jnp.float32
BF16 = jnp.bfloat16

EPS = 1e-6
LANES = 128
SUBLANES = 8
MXU_WIDTH = 256
SSM_GROUP = 16
SSM_CHUNK = 8
CONV_HALO = 2 * SUBLANES
VMEM_LIMIT = 56 * 1024 * 1024
ROW_TILE = 512
FFN_TF = 512
FFN_PARTS = 4


def _cparams(*sem):
    return pltpu.CompilerParams(dimension_semantics=sem, vmem_limit_bytes=VMEM_LIMIT)


def _dot(a, b):
    return jnp.dot(a, b, preferred_element_type=F32)


def _dot_nt(a, b):
    return lax.dot_general(a, b, (((1,), (1,)), ((), ())), preferred_element_type=F32)


def _resident(shape):
    return pl.BlockSpec(shape, lambda *_: (0,) * len(shape), pipeline_mode=pl.Buffered(1))


def _ada_kernel(c_ref, w_ref, b_ref, o_ref):
    c = c_ref[...]
    o_ref[...] = _dot(c * jax.nn.sigmoid(c), w_ref[...]) + b_ref[...]


def _ada(c, w, b, tn=1024):
    bsz, d = c.shape
    n = w.shape[1]
    return pl.pallas_call(
        _ada_kernel,
        out_shape=jax.ShapeDtypeStruct((bsz, n), F32),
        grid=(n // tn,),
        in_specs=[pl.BlockSpec((bsz, d), lambda j: (0, 0)),
                  pl.BlockSpec((d, tn), lambda j: (0, j)),
                  pl.BlockSpec((1, tn), lambda j: (0, j))],
        out_specs=pl.BlockSpec((bsz, tn), lambda j: (0, j)),
        compiler_params=_cparams("arbitrary"),
        name="ada_ln",
    )(c, w, b.reshape(1, n))


def _norm_modulate(x, g, shift, scale):
    y = x * lax.rsqrt(jnp.mean(x * x, axis=-1, keepdims=True) + EPS)
    return y * g * (1.0 + scale) + shift


def _half_rms_norm(a, gain, post_scale):
    outs = []
    half = LANES // 2
    for cidx in range(a.shape[1] // LANES):
        blk = a[:, cidx * LANES:(cidx + 1) * LANES]
        sq = blk * blk
        lo = lax.broadcasted_iota(jnp.int32, blk.shape, 1) < half
        s_lo = jnp.sum(jnp.where(lo, sq, 0.0), axis=-1, keepdims=True)
        s_hi = jnp.sum(jnp.where(lo, 0.0, sq), axis=-1, keepdims=True)
        ms = jnp.where(lo, s_lo, s_hi) * (1.0 / half)
        outs.append(blk * lax.rsqrt(ms + EPS) * (gain * post_scale))
    return jnp.concatenate(outs, axis=-1)


def _inproj_kernel(x_ref, g_ref, shift_ref, scale_ref, w_ref, qg_ref, kg_ref,
                   qkv_ref, u_ref, *, q_scale):
    aw = u_ref.shape[1]
    h = _norm_modulate(x_ref[...], g_ref[...], shift_ref[...], scale_ref[...]).astype(BF16)
    qkv_ref[:, :aw] = _half_rms_norm(_dot(h, w_ref[:, :aw]), qg_ref[...], q_scale).astype(BF16)
    qkv_ref[:, aw:2 * aw] = _half_rms_norm(_dot(h, w_ref[:, aw:2 * aw]), kg_ref[...], 1.0).astype(BF16)
    qkv_ref[:, 2 * aw:] = _dot(h, w_ref[:, 2 * aw:3 * aw]).astype(BF16)
    u_ref[...] = _dot(h, w_ref[:, 3 * aw:])


def _inproj(x2, mod4, norm_g, w_bf, qg, kg, seq, tm):
    t, d = x2.shape
    aw = w_bf.shape[1] // 4
    tiles_per_seq = seq // tm
    bidx = lambda i: i // tiles_per_seq
    qg2 = jnp.tile(qg, 2).reshape(1, LANES)
    kg2 = jnp.tile(kg, 2).reshape(1, LANES)
    head_dim = qg.shape[0]
    return pl.pallas_call(
        functools.partial(_inproj_kernel, q_scale=head_dim ** -0.5 * math.log2(math.e)),
        out_shape=(jax.ShapeDtypeStruct((t, 3 * aw), BF16), jax.ShapeDtypeStruct((t, aw), F32)),
        grid=(t // tm,),
        in_specs=[pl.BlockSpec((tm, d), lambda i: (i, 0)),
                  pl.BlockSpec((1, d), lambda i: (0, 0)),
                  pl.BlockSpec((None, None, 1, d), lambda i: (bidx(i), 0, 0, 0)),
                  pl.BlockSpec((None, None, 1, d), lambda i: (bidx(i), 1, 0, 0)),
                  _resident(w_bf.shape),
                  pl.BlockSpec((1, LANES), lambda i: (0, 0)),
                  pl.BlockSpec((1, LANES), lambda i: (0, 0))],
        out_specs=(pl.BlockSpec((tm, 3 * aw), lambda i: (i, 0)),
                   pl.BlockSpec((tm, aw), lambda i: (i, 0))),
        compiler_params=_cparams("arbitrary"),
        name="in_proj",
    )(x2, norm_g.reshape(1, d), mod4, mod4, w_bf, qg2, kg2)


ATTN_TQ = 512
ATTN_TK = 512
ATTN_ROWS = 64
ATTN_HEADS_PER_STEP = 1
NEG_BIG = -1e30
POS_SPLIT = 64


def _attn_kernel(lamv_ref, gsub_ref, coef_ref, pos_ref, q_ref, k_ref, v_ref, o_ref,
                 kaug_ref, vaug_ref, qaug_ref, s_ref, p_ref, m_ref, alpha_ref, acc_ref,
                 *, tq, lambda_init):
    i = pl.program_id(2)
    half = LANES // 2
    heads = range(ATTN_HEADS_PER_STEP)
    head_lanes = [slice(hd * LANES, (hd + 1) * LANES) for hd in heads]

    @pl.when(i == 0)
    def _():
        for hd in heads:
            kaug_ref[hd, :, :LANES] = k_ref[:, head_lanes[hd]]
            kaug_ref[hd, :, LANES:] = pos_ref[...]
            vaug_ref[hd, :, :LANES] = v_ref[:, head_lanes[hd]]
            vaug_ref[hd, :, LANES:] = jnp.ones((v_ref.shape[0], LANES), BF16)

    for hd in heads:
        q = q_ref[:, head_lanes[hd]]
        lane = lax.broadcasted_iota(jnp.int32, q.shape, 1)
        zero = jnp.zeros_like(q)
        qaug_ref[hd, 0, :, :LANES] = jnp.where(lane < half, q, zero)
        qaug_ref[hd, 1, :, :LANES] = jnp.where(lane >= half, q, zero)
        coef = jnp.broadcast_to(coef_ref[hd], q.shape)
        qaug_ref[hd, 0, :, LANES:] = coef
        qaug_ref[hd, 1, :, LANES:] = coef

    m_ref[...] = jnp.full(m_ref.shape, NEG_BIG, F32)
    acc_ref[...] = jnp.zeros(acc_ref.shape, F32)
    alpha_ref[:, 1] = jnp.ones((len(heads),) + alpha_ref.shape[2:], F32)
    p_ref[:, 1] = jnp.zeros((len(heads),) + p_ref.shape[2:], BF16)

    nsub = tq // ATTN_ROWS
    tk = s_ref.shape[-1]
    hk = tk // 2
    per_tile = tq // tk

    def scores(j, idx, part, hd):
        start = pl.multiple_of(j * tk + part * hk, hk)
        s_ref[hd, idx, :, part * hk:(part + 1) * hk] = _dot_nt(qaug_ref[hd, idx],
                                                               kaug_ref[hd, pl.ds(start, hk), :])

    def softmax(idx, mask_off, chunks, hd):
        for r in chunks:
            rows = slice(r * ATTN_ROWS, (r + 1) * ATTN_ROWS)
            s = s_ref[hd, idx, rows, :]
            if mask_off is not None:
                row = lax.broadcasted_iota(jnp.int32, s.shape, 0) + r * ATTN_ROWS
                col = lax.broadcasted_iota(jnp.int32, s.shape, 1) + mask_off
                s = jnp.where(col <= row, s, NEG_BIG)
            m_old = m_ref[hd, idx, rows, :]
            m_new = jnp.maximum(m_old, jnp.max(s, axis=-1, keepdims=True))
            alpha_ref[hd, idx, rows, :] = jnp.exp2(m_old - m_new)
            m_ref[hd, idx, rows, :] = m_new
            p_ref[hd, idx, rows, :] = jnp.exp2(s - jnp.tile(m_new, (1, tk // LANES))).astype(BF16)

    def values(j, idx, hd):
        start = pl.multiple_of(jnp.maximum(j, 0) * tk, tk)
        pv = _dot(p_ref[hd, idx], vaug_ref[hd, pl.ds(start, tk), :])
        acc_ref[hd, idx] = jnp.tile(alpha_ref[hd, idx], (1, 2)) * acc_ref[hd, idx] + pv

    def unit(j_s, idx_s, idx_sm, mask_off, j_v, idx_v):
        matmuls = [functools.partial(values, j_v, idx_v)]
        if j_s is not None:
            matmuls += [functools.partial(scores, j_s, idx_s, part) for part in range(2)]
        bounds = [round(k * nsub / len(matmuls)) for k in range(len(matmuls) + 1)]
        for k, matmul in enumerate(matmuls):
            for hd in heads:
                matmul(hd)
            for hd in heads:
                softmax(idx_sm, mask_off, range(bounds[k], bounds[k + 1]), hd)

    for part in range(2):
        for hd in heads:
            scores(0, 0, part, hd)

    def block(j):
        unit(j, 1, 0, None, j - 1, 1)
        unit(j + 1, 0, 1, None, j, 0)

    def four_blocks(jj, carry):
        for t in range(4):
            block(4 * jj + t)
        return carry

    nfull = i * per_tile
    lax.fori_loop(0, nfull // 4, four_blocks, 0)

    @pl.when(nfull % 4 >= 2)
    def _():
        block(nfull // 4 * 4)
        block(nfull // 4 * 4 + 1)

    @pl.when(nfull % 2 == 1)
    def _():
        block(nfull - 1)

    tail = [(nfull + t, idx, t * tk) for t in range(per_tile) for idx in range(2)]
    j_v, idx_v = nfull - 1, 1
    for n, (j_sm, idx_sm, off) in enumerate(tail):
        j_s, idx_s = tail[n + 1][:2] if n + 1 < len(tail) else (None, None)
        unit(j_s, idx_s, idx_sm, off, j_v, idx_v)
        j_v, idx_v = j_sm, idx_sm
    for hd in heads:
        values(j_v, idx_v, hd)

    lv = lamv_ref[...]
    lam = (jnp.exp(jnp.sum(lv[0:1] * lv[1:2], axis=-1, keepdims=True))
           - jnp.exp(jnp.sum(lv[2:3] * lv[3:4], axis=-1, keepdims=True)) + lambda_init)
    for hd in heads:
        a0 = acc_ref[hd, 0]
        a1 = acc_ref[hd, 1]
        o = a0[:, :LANES] / a0[:, LANES:] - lam * (a1[:, :LANES] / a1[:, LANES:])
        o = o * lax.rsqrt(jnp.mean(o * o, axis=-1, keepdims=True) + EPS)
        o_ref[:, head_lanes[hd]] = (o * gsub_ref[...] * (1.0 - lambda_init)).astype(BF16)


def _split_bf16(x, n):
    parts = []
    for _ in range(n):
        p = x.astype(BF16)
        parts.append(p)
        x = x - p.astype(F32)
    return parts


def _attention(qkv, lamv, gsub, bsz, seq, lambda_init, tq, tk):
    t = qkv.shape[0]
    aw = qkv.shape[1] // 3
    heads = aw // LANES
    hps = ATTN_HEADS_PER_STEP
    groups = heads // hps
    nq = seq // tq
    c = jnp.asarray([2.0 ** (-8.0 * (i + 1) / heads) * math.log2(math.e) for i in range(heads)], F32)
    cparts = _split_bf16(c, 3)
    coef = jnp.stack([p * POS_SPLIT for p in cparts] + cparts, axis=-1)
    coef = jnp.pad(coef, ((0, 0), (0, LANES - coef.shape[1]))).reshape(heads, 1, LANES)
    kpos = jnp.arange(seq, dtype=jnp.int32)
    hi = (kpos // POS_SPLIT).astype(BF16)
    lo = (kpos % POS_SPLIT).astype(BF16)
    pos = jnp.pad(jnp.stack([hi, hi, hi, lo, lo, lo], axis=-1), ((0, 0), (0, LANES - 6)))
    return pl.pallas_call(
        functools.partial(_attn_kernel, tq=tq, lambda_init=lambda_init),
        out_shape=jax.ShapeDtypeStruct((t, aw), BF16),
        grid=(bsz, groups, nq),
        in_specs=[pl.BlockSpec((4, LANES // 2), lambda b, h, i: (0, 0)),
                  pl.BlockSpec((1, LANES), lambda b, h, i: (0, 0)),
                  pl.BlockSpec((hps, 1, LANES), lambda b, h, i: (h, 0, 0)),
                  pl.BlockSpec((seq, LANES), lambda b, h, i: (0, 0)),
                  pl.BlockSpec((tq, hps * LANES), lambda b, h, i: (b * nq + i, h)),
                  pl.BlockSpec((seq, hps * LANES), lambda b, h, i: (b, groups + h)),
                  pl.BlockSpec((seq, hps * LANES), lambda b, h, i: (b, 2 * groups + h))],
        out_specs=pl.BlockSpec((tq, hps * LANES), lambda b, h, i: (b * nq + i, h)),
        scratch_shapes=[pltpu.VMEM((hps, seq, 2 * LANES), BF16), pltpu.VMEM((hps, seq, 2 * LANES), BF16),
                        pltpu.VMEM((hps, 2, tq, 2 * LANES), BF16),
                        pltpu.VMEM((hps, 2, tq, tk), F32), pltpu.VMEM((hps, 2, tq, tk), BF16),
                        pltpu.VMEM((hps, 2, tq, LANES), F32), pltpu.VMEM((hps, 2, tq, LANES), F32),
                        pltpu.VMEM((hps, 2, tq, 2 * LANES), F32)],
        compiler_params=_cparams("arbitrary", "arbitrary", "arbitrary"),
        name="diff_attn",
    )(lamv, gsub.reshape(1, LANES), coef, pos, qkv, qkv, qkv)


def _ssm_matrix_kernel(c_re_ref, c_im_ref, bt_re_ref, bt_im_ref, pw_re_ref, pw_im_ref,
                       tm_ref, bd_ref, cd_ref):
    q = SSM_CHUNK
    nrow, p = c_re_ref.shape
    width = pw_re_ref.shape[1]
    gb = width // p
    row_g = lax.broadcasted_iota(jnp.int32, (nrow, width), 0) // (nrow // gb)
    col_g = lax.broadcasted_iota(jnp.int32, (nrow, width), 1) // p
    same_group = row_g == col_g

    def expand(ref):
        return jnp.where(same_group, jnp.tile(ref[...], (1, gb)), 0.0)

    def times_power(xr, xi, n):
        pr = pw_re_ref[n:n + 1, :]
        pi = pw_im_ref[n:n + 1, :]
        return xr * pr - xi * pi, xr * pi + xi * pr

    def dot_nt_f32(a, b):
        a_hi, b_hi = a.astype(BF16), b.astype(BF16)
        a_lo = (a - a_hi.astype(F32)).astype(BF16)
        b_lo = (b - b_hi.astype(F32)).astype(BF16)
        return _dot_nt(a_hi, b_hi) + _dot_nt(a_hi, b_lo) + _dot_nt(a_lo, b_hi)

    c_re, c_im = expand(c_re_ref), expand(c_im_ref)
    bt_re, bt_im = expand(bt_re_ref), expand(bt_im_ref)

    tm_ref[...] = jnp.zeros(tm_ref.shape, BF16)
    for tau in range(q):
        ca_re, ca_im = times_power(c_re, c_im, tau)
        blk = (dot_nt_f32(bt_re, ca_re) - dot_nt_f32(bt_im, ca_im)).astype(BF16)
        for jp in range(q - tau):
            tm_ref[jp * LANES:(jp + 1) * LANES, (jp + tau) * LANES:(jp + tau + 1) * LANES] = blk
    for jp in range(q):
        e_re, e_im = times_power(bt_re, bt_im, q - 1 - jp)
        bd_ref[jp * LANES:(jp + 1) * LANES, :width] = e_re.astype(BF16)
        bd_ref[jp * LANES:(jp + 1) * LANES, width:] = e_im.astype(BF16)
    for j in range(q):
        f_re, f_im = times_power(c_re, c_im, j + 1)
        cd_ref[:width, j * LANES:(j + 1) * LANES] = f_re.T.astype(BF16)
        cd_ref[width:, j * LANES:(j + 1) * LANES] = (-f_im).T.astype(BF16)


def _ssm_matrices(a_re, a_im, log_dt, b_re, b_im, c_re, c_im, nk):
    g, p = a_re.shape
    cch = b_re.shape[-1]
    q = SSM_CHUNK
    gb = LANES // cch
    nb = g // gb
    dt = jnp.exp(log_dt.astype(F32))[:, None]
    ar, ai = a_re.astype(F32), a_im.astype(F32)
    mag = jnp.exp(dt * ar)
    abar_re, abar_im = mag * jnp.cos(dt * ai), mag * jnp.sin(dt * ai)
    den = ar * ar + ai * ai
    nr, ni = abar_re - 1.0, abar_im
    coef_re = (nr * ar + ni * ai) / den
    coef_im = (ni * ar - nr * ai) / den
    br, bi = b_re.astype(F32), b_im.astype(F32)
    bbar_re = coef_re[..., None] * br - coef_im[..., None] * bi
    bbar_im = coef_re[..., None] * bi + coef_im[..., None] * br

    def power(n):
        n = jnp.asarray(n, F32)[:, None, None]
        m = jnp.exp(n * (dt * ar)[None])
        return m * jnp.cos(n * (dt * ai)[None]), m * jnp.sin(n * (dt * ai)[None])

    rows_c = lambda x: x.reshape(g * cch, p)
    c2 = [rows_c(c_re.astype(F32)), rows_c(c_im.astype(F32))]
    bt2 = [rows_c(bbar_re.transpose(0, 2, 1)), rows_c(bbar_im.transpose(0, 2, 1))]
    pw = [x.reshape(q + 1, g * p) for x in power(range(q + 1))]
    width = gb * p
    side = q * LANES
    par = pl.BlockSpec((LANES, p), lambda i: (i, 0))
    pws = pl.BlockSpec((q + 1, width), lambda i: (0, i))
    mat = lambda r, c: pl.BlockSpec((None, r, c), lambda i: (i, 0, 0))
    tmat, bd, cd = pl.pallas_call(
        _ssm_matrix_kernel,
        out_shape=(jax.ShapeDtypeStruct((nb, side, side), BF16),
                   jax.ShapeDtypeStruct((nb, side, 2 * width), BF16),
                   jax.ShapeDtypeStruct((nb, 2 * width, side), BF16)),
        grid=(nb,),
        in_specs=[par, par, par, par, pws, pws],
        out_specs=(mat(side, side), mat(side, 2 * width), mat(2 * width, side)),
        compiler_params=_cparams("arbitrary"),
        name="s5_matrices",
    )(c2[0], c2[1], bt2[0], bt2[1], pw[0], pw[1])

    def lanes(x):
        return x.reshape(x.shape[0], nb, width).transpose(1, 0, 2)

    aq = [lanes(x) for x in power([q])]
    aseg = [lanes(x) for x in power([q * nk])]
    return tmat, bd, cd, aq, aseg


def _ssm_kernel(u_ref, tm_ref, bd_ref, cd_ref, aqr_ref, aqi_ref, asr_ref, asi_ref,
                d_ref, y_ref, z_ref, yi_ref, sre_ref, sim_ref, xre_ref, xim_ref, xn_ref, *, bsz, nk):
    q = SSM_CHUNK
    rows = z_ref.shape[0]
    nslab = sre_ref.shape[0]
    half = nslab * LANES
    nseq = rows // nk
    nseg = nseq // bsz
    step_n = MXU_WIDTH

    for j in range(q):
        z_ref[:, j * LANES:(j + 1) * LANES] = u_ref[pl.ds(j, rows, stride=q), :].astype(BF16)
    z = z_ref[...]
    for c in range(0, q * LANES, step_n):
        yi_ref[:, c:c + step_n] = _dot(z, tm_ref[:, c:c + step_n])
    for c in range(0, 2 * half, step_n):
        s = _dot(z, bd_ref[:, c:c + step_n])
        dst = sre_ref if c < half else sim_ref
        for hh in range(step_n // LANES):
            slab = (c % half) // LANES + hh
            for sq in range(nseq):
                dst[slab, pl.ds(sq, nk, stride=nseq), :] = s[sq * nk:(sq + 1) * nk, hh * LANES:(hh + 1) * LANES]

    ar = [jnp.broadcast_to(aqr_ref[:, s * LANES:(s + 1) * LANES], (nseq, LANES)) for s in range(nslab)]
    ai = [jnp.broadcast_to(aqi_ref[:, s * LANES:(s + 1) * LANES], (nseq, LANES)) for s in range(nslab)]

    def step(k, carry):
        xr, xi = carry
        base = pl.multiple_of(k * nseq, nseq)
        nr, ni = [], []
        for s in range(nslab):
            xre_ref[s, pl.ds(base, nseq), :] = xr[s]
            xim_ref[s, pl.ds(base, nseq), :] = xi[s]
            nr.append(ar[s] * xr[s] - ai[s] * xi[s] + sre_ref[s, pl.ds(base, nseq), :])
            ni.append(ar[s] * xi[s] + ai[s] * xr[s] + sim_ref[s, pl.ds(base, nseq), :])
        return tuple(nr), tuple(ni)

    zero = tuple(jnp.zeros((nseq, LANES), F32) for _ in range(nslab))
    er, ei = lax.fori_loop(0, nk, step, (zero, zero), unroll=2)

    cr, ci = [], []
    for s in range(nslab):
        asr = asr_ref[:, s * LANES:(s + 1) * LANES]
        asi = asi_ref[:, s * LANES:(s + 1) * LANES]
        rows_r, rows_i = [], []
        for b in range(bsz):
            cre = jnp.zeros((1, LANES), F32)
            cim = jnp.zeros((1, LANES), F32)
            for sg in range(nseg):
                rows_r.append(cre)
                rows_i.append(cim)
                sq = b * nseg + sg
                cre, cim = (er[s][sq:sq + 1] + asr * cre - asi * cim,
                            ei[s][sq:sq + 1] + asr * cim + asi * cre)
        cr.append(jnp.concatenate(rows_r, axis=0))
        ci.append(jnp.concatenate(rows_i, axis=0))

    def fix(k, carry):
        wr, wi = carry
        base = pl.multiple_of(k * nseq, nseq)
        nr, ni = [], []
        for s in range(nslab):
            xre_ref[s, pl.ds(base, nseq), :] += wr[s]
            xim_ref[s, pl.ds(base, nseq), :] += wi[s]
            nr.append(ar[s] * wr[s] - ai[s] * wi[s])
            ni.append(ar[s] * wi[s] + ai[s] * wr[s])
        return tuple(nr), tuple(ni)

    lax.fori_loop(0, nk, fix, (tuple(cr), tuple(ci)), unroll=2)

    for s in range(nslab):
        for sq in range(nseq):
            rs = slice(sq * nk, (sq + 1) * nk)
            xn_ref[rs, s * LANES:(s + 1) * LANES] = xre_ref[s, pl.ds(sq, nk, stride=nseq), :].astype(BF16)
            xn_ref[rs, half + s * LANES:half + (s + 1) * LANES] = (
                xim_ref[s, pl.ds(sq, nk, stride=nseq), :].astype(BF16))
    xn = xn_ref[...]
    d = d_ref[...]
    for c in range(0, q * LANES, step_n):
        yy = yi_ref[:, c:c + step_n] + _dot(xn, cd_ref[:, c:c + step_n])
        for hh in range(step_n // LANES):
            j = c // LANES + hh
            y_ref[pl.ds(j, rows, stride=q), :] = (yy[:, hh * LANES:(hh + 1) * LANES]
                                                  + d * u_ref[pl.ds(j, rows, stride=q), :])


def _ssm(u2, a_re, a_im, log_dt, b_re, b_im, c_re, c_im, d_skip, bsz):
    t, sw = u2.shape
    q = SSM_CHUNK
    rows = t // q
    assert SUBLANES % bsz == 0 and rows % SUBLANES == 0
    nk = rows // SUBLANES
    tmat, bd, cd, aq, aseg = _ssm_matrices(a_re, a_im, log_dt, b_re, b_im, c_re, c_im, nk)
    nb, _, st = bd.shape
    half = st // 2
    nslab = half // LANES
    mat = lambda shape: pl.BlockSpec((None,) + shape, lambda i: (i, 0, 0))
    return pl.pallas_call(
        functools.partial(_ssm_kernel, bsz=bsz, nk=nk),
        out_shape=jax.ShapeDtypeStruct((t, sw), F32),
        grid=(nb,),
        in_specs=[pl.BlockSpec((t, LANES), lambda i: (0, i)),
                  mat((q * LANES, q * LANES)), mat((q * LANES, st)), mat((st, q * LANES)),
                  mat((1, half)), mat((1, half)), mat((1, half)), mat((1, half)),
                  pl.BlockSpec((1, LANES), lambda i: (0, i))],
        out_specs=pl.BlockSpec((t, LANES), lambda i: (0, i)),
        scratch_shapes=[pltpu.VMEM((rows, q * LANES), BF16), pltpu.VMEM((rows, q * LANES), F32)]
                       + [pltpu.VMEM((nslab, rows, LANES), F32) for _ in range(4)]
                       + [pltpu.VMEM((rows, st), BF16)],
        compiler_params=_cparams("arbitrary"),
        name="s5_scan",
    )(u2, tmat, bd, cd, aq[0], aq[1], aseg[0], aseg[1],
      d_skip.astype(F32).reshape(1, sw))


def _mix_kernel(x_ref, attn_ref, y_ref, wglu_ref, bglu_ref, wout_ref, gate_ref,
                g_ref, shift_ref, scale_ref, x1_ref, h2_ref):
    aw = attn_ref.shape[1]
    yg = jax.nn.gelu(y_ref[...])
    z = _dot(yg.astype(BF16), wglu_ref[...]) + bglu_ref[...]
    yy = yg * jax.nn.sigmoid(z)
    mix = _dot(attn_ref[...], wout_ref[:aw, :]) + _dot(yy.astype(BF16), wout_ref[aw:, :])
    x1 = x_ref[...] + gate_ref[...] * mix
    x1_ref[...] = x1
    h2_ref[...] = _norm_modulate(x1, g_ref[...], shift_ref[...], scale_ref[...]).astype(BF16)


def _mix(x2, attn, y, wglu_bf, bglu, wout_bf, mod4, norm_g, seq, tm):
    t, d = x2.shape
    aw = attn.shape[1]
    tiles_per_seq = seq // tm
    bidx = lambda i: i // tiles_per_seq
    modspec = lambda k: pl.BlockSpec((None, None, 1, d), lambda i: (bidx(i), k, 0, 0))
    return pl.pallas_call(
        _mix_kernel,
        out_shape=(jax.ShapeDtypeStruct((t, d), F32), jax.ShapeDtypeStruct((t, d), BF16)),
        grid=(t // tm,),
        in_specs=[pl.BlockSpec((tm, d), lambda i: (i, 0)),
                  pl.BlockSpec((tm, aw), lambda i: (i, 0)),
                  pl.BlockSpec((tm, aw), lambda i: (i, 0)),
                  _resident(wglu_bf.shape),
                  pl.BlockSpec((1, aw), lambda i: (0, 0)),
                  _resident(wout_bf.shape),
                  modspec(2),
                  pl.BlockSpec((1, d), lambda i: (0, 0)),
                  modspec(3), modspec(4)],
        out_specs=(pl.BlockSpec((tm, d), lambda i: (i, 0)), pl.BlockSpec((tm, d), lambda i: (i, 0))),
        compiler_params=_cparams("arbitrary"),
        name="mix_out_proj",
    )(x2, attn, y, wglu_bf, bglu.reshape(1, aw), wout_bf, mod4, norm_g.reshape(1, d), mod4, mod4)


def _ffn_kernel(h_ref, halo_ref, wa_ref, wg_ref, cw_ref, cb_ref, wd_ref, x1_ref, gate_ref,
                o_ref, a0_ref, a1_ref, g0_ref, g1_ref, act0_ref, act1_ref, acc_ref,
                *, tiles_per_seq, nf, units):
    n = pl.program_id(0)
    slot = n % 2
    f_down = jnp.maximum(n - 2, 0) % nf
    tm = h_ref.shape[0]

    @pl.when(n == 0)
    def _():
        a1_ref[...] = jnp.zeros(a1_ref.shape, F32)
        g1_ref[...] = jnp.zeros(g1_ref.shape, F32)
        act0_ref[...] = jnp.zeros(act0_ref.shape, BF16)

    @pl.when(f_down == 0)
    def _():
        acc_ref[...] = jnp.zeros(acc_ref.shape, F32)

    row_tile = jnp.minimum(n, units - 1) // nf

    def stages(a_rd, g_rd, a_wr, g_wr, act_rd, act_wr):
        halo = jnp.where(row_tile % tiles_per_seq == 0, jnp.zeros_like(halo_ref[...]), halo_ref[...])
        h = h_ref[...]
        h_ext = jnp.concatenate([halo, h], axis=0)
        act = act_rd[...]
        cw = cw_ref[...]
        cb = cb_ref[...]
        tf = a_wr.shape[1]
        d = acc_ref.shape[1]

        def down(c):
            cols = slice(c * (d // FFN_PARTS), (c + 1) * (d // FFN_PARTS))
            acc_ref[:, cols] += _dot(act, wd_ref[:, cols])

        def up(c, lhs, w_ref, dst):
            cols = slice(c * (2 * tf // FFN_PARTS), (c + 1) * (2 * tf // FFN_PARTS))
            dst[:, cols] = _dot(lhs, w_ref[:, cols])

        def conv_gelu(r):
            nrow = tm // (2 * FFN_PARTS)
            r0 = CONV_HALO + r * nrow
            conv = (cw[2:3] * a_rd[r0:r0 + nrow, :] + cw[1:2] * a_rd[r0 - 1:r0 - 1 + nrow, :]
                    + cw[0:1] * a_rd[r0 - 2:r0 - 2 + nrow, :] + cb)
            act_wr[r * nrow:(r + 1) * nrow, :] = (
                jax.nn.gelu(conv) * g_rd[r * nrow:(r + 1) * nrow, :]).astype(BF16)

        mxu_work = ([functools.partial(down, c) for c in range(2)]
                    + [functools.partial(up, 0, h_ext, wa_ref, a_wr), functools.partial(down, 2),
                       functools.partial(up, 0, h, wg_ref, g_wr), functools.partial(down, 3),
                       functools.partial(up, 1, h_ext, wa_ref, a_wr), functools.partial(up, 1, h, wg_ref, g_wr)])
        for r, matmul in enumerate(mxu_work):
            matmul()
            conv_gelu(r)

    @pl.when(slot == 0)
    def _():
        stages(a1_ref, g1_ref, a0_ref, g0_ref, act0_ref, act1_ref)

    @pl.when(slot == 1)
    def _():
        stages(a0_ref, g0_ref, a1_ref, g1_ref, act1_ref, act0_ref)

    @pl.when((f_down == nf - 1) & (n > 1))
    def _():
        o_ref[...] = x1_ref[...] + gate_ref[...] * acc_ref[...]


def _ffn(h2, x1, wup_bf, conv_w, conv_b, wdown_bf, mod4, seq, tm, tf):
    t, d = h2.shape
    dff = wdown_bf.shape[0]
    nf = dff // tf
    tiles_per_seq = seq // tm
    halo_blocks = tm // CONV_HALO
    units = (t // tm) * nf
    cur_tile = lambda n: jnp.minimum(n, units - 1) // nf
    cur_blk = lambda n: jnp.minimum(n, units - 1) % nf
    conv_blk = lambda n: jnp.clip(n - 1, 0, units - 1) % nf
    down_tile = lambda n: jnp.maximum(n - 2, 0) // nf
    down_blk = lambda n: jnp.maximum(n - 2, 0) % nf
    return pl.pallas_call(
        functools.partial(_ffn_kernel, tiles_per_seq=tiles_per_seq, nf=nf, units=units),
        out_shape=jax.ShapeDtypeStruct((t, d), F32),
        grid=(units + 2,),
        in_specs=[pl.BlockSpec((tm, d), lambda n: (cur_tile(n), 0)),
                  pl.BlockSpec((CONV_HALO, d), lambda n: (jnp.maximum(cur_tile(n) * halo_blocks - 1, 0), 0)),
                  pl.BlockSpec((d, tf), lambda n: (0, cur_blk(n))),
                  pl.BlockSpec((d, tf), lambda n: (0, nf + cur_blk(n))),
                  pl.BlockSpec((conv_w.shape[0], tf), lambda n: (0, conv_blk(n))),
                  pl.BlockSpec((1, tf), lambda n: (0, conv_blk(n))),
                  pl.BlockSpec((tf, d), lambda n: (down_blk(n), 0)),
                  pl.BlockSpec((tm, d), lambda n: (down_tile(n), 0)),
                  pl.BlockSpec((None, None, 1, d), lambda n: (down_tile(n) // tiles_per_seq, 5, 0, 0))],
        out_specs=pl.BlockSpec((tm, d), lambda n: (down_tile(n), 0)),
        scratch_shapes=[pltpu.VMEM((tm + CONV_HALO, tf), F32), pltpu.VMEM((tm + CONV_HALO, tf), F32),
                        pltpu.VMEM((tm, tf), F32), pltpu.VMEM((tm, tf), F32),
                        pltpu.VMEM((tm, tf), BF16), pltpu.VMEM((tm, tf), BF16),
                        pltpu.VMEM((tm, d), F32)],
        compiler_params=_cparams("arbitrary"),
        name="conv_ffn",
    )(h2, h2, wup_bf, wup_bf, conv_w, conv_b.reshape(1, dff), wdown_bf, x1, mod4)


def kernel(x, c, w_ada, b_ada, norm1_g, norm2_g, w_in, q_norm_g, k_norm_g, lambda_q1, lambda_k1,
           lambda_q2, lambda_k2, attn_sub_norm_g, ssm_a_re, ssm_a_im, ssm_log_dt, ssm_b_re, ssm_b_im,
           ssm_c_re, ssm_c_im, ssm_d, w_glu, b_glu, w_out, w_up, conv_w, conv_b, w_down):
    bsz, seq, d = x.shape
    depth = w_ada.shape[0]
    tm = min(ROW_TILE, seq)
    x2 = x.reshape(bsz * seq, d)
    for l in range(depth):
        lambda_init = 0.8 - 0.6 * math.exp(-0.3 * l)
        mod = _ada(c, w_ada[l], b_ada[l])
        mod4 = mod.reshape(bsz, 6, 1, d)
        qkv, u = _inproj(x2, mod4, norm1_g[l], w_in[l].astype(BF16), q_norm_g[l], k_norm_g[l], seq, tm)
        lamv = jnp.stack([lambda_q1[l], lambda_k1[l], lambda_q2[l], lambda_k2[l]]).astype(F32)
        attn = _attention(qkv, lamv, attn_sub_norm_g[l], bsz, seq, lambda_init,
                          min(ATTN_TQ, seq), min(ATTN_TK, seq))
        y = _ssm(u, ssm_a_re[l], ssm_a_im[l], ssm_log_dt[l], ssm_b_re[l], ssm_b_im[l],
                 ssm_c_re[l], ssm_c_im[l], ssm_d[l], bsz)
        x1, h2 = _mix(x2, attn, y, w_glu[l].astype(BF16), b_glu[l], w_out[l].astype(BF16), mod4,
                      norm2_g[l], seq, tm)
        x2 = _ffn(h2, x1, w_up[l].astype(BF16), conv_w[l], conv_b[l], w_down[l].astype(BF16), mod4,
                  seq, tm, FFN_TF)
    return x2.reshape(bsz, seq, d)
```

```python
import functools
import math

import jax
import jax.numpy as jnp
from jax import lax
from jax.experimental import pallas as pl
from jax.experimental.pallas import tpu as pltpu

F32 = jnp.float32
BF16 = jnp.bfloat16

EPS = 1e-6
LANES = 128
SUBLANES = 8
MXU_WIDTH = 256
SSM_GROUP = 16
SSM_CHUNK = 8
CONV_HALO = 2 * SUBLANES
VMEM_LIMIT = 56 * 1024 * 1024
ROW_TILE = 512
FFN_TF = 512
FFN_PARTS = 4


def _cparams(*sem):
    return pltpu.CompilerParams(dimension_semantics=sem, vmem_limit_bytes=VMEM_LIMIT)


def _dot(a, b):
    return jnp.dot(a, b, preferred_element_type=F32)


def _dot_nt(a, b):
    return lax.dot_general(a, b, (((1,), (1,)), ((), ())), preferred_element_type=F32)


def _resident(shape):
    return pl.BlockSpec(shape, lambda *_: (0,) * len(shape), pipeline_mode=pl.Buffered(1))


def _ada_kernel(c_ref, w_ref, b_ref, o_ref):
    c = c_ref[...]
    o_ref[...] = _dot(c * jax.nn.sigmoid(c), w_ref[...]) + b_ref[...]


def _ada(c, w, b, tn=1024):
    bsz, d = c.shape
    n = w.shape[1]
    return pl.pallas_call(
        _ada_kernel,
        out_shape=jax.ShapeDtypeStruct((bsz, n), F32),
        grid=(n // tn,),
        in_specs=[pl.BlockSpec((bsz, d), lambda j: (0, 0)),
                  pl.BlockSpec((d, tn), lambda j: (0, j)),
                  pl.BlockSpec((1, tn), lambda j: (0, j))],
        out_specs=pl.BlockSpec((bsz, tn), lambda j: (0, j)),
        compiler_params=_cparams("arbitrary"),
        name="ada_ln",
    )(c, w, b.reshape(1, n))


def _norm_modulate(x, g, shift, scale):
    y = x * lax.rsqrt(jnp.mean(x * x, axis=-1, keepdims=True) + EPS)
    return y * g * (1.0 + scale) + shift


def _half_rms_norm(a, gain, post_scale):
    outs = []
    half = LANES // 2
    for cidx in range(a.shape[1] // LANES):
        blk = a[:, cidx * LANES:(cidx + 1) * LANES]
        sq = blk * blk
        lo = lax.broadcasted_iota(jnp.int32, blk.shape, 1) < half
        s_lo = jnp.sum(jnp.where(lo, sq, 0.0), axis=-1, keepdims=True)
        s_hi = jnp.sum(jnp.where(lo, 0.0, sq), axis=-1, keepdims=True)
        ms = jnp.where(lo, s_lo, s_hi) * (1.0 / half)
        outs.append(blk * lax.rsqrt(ms + EPS) * (gain * post_scale))
    return jnp.concatenate(outs, axis=-1)


def _inproj_kernel(x_ref, g_ref, shift_ref, scale_ref, w_ref, qg_ref, kg_ref,
                   qkv_ref, u_ref, *, q_scale):
    aw = u_ref.shape[1]
    h = _norm_modulate(x_ref[...], g_ref[...], shift_ref[...], scale_ref[...]).astype(BF16)
    qkv_ref[:, :aw] = _half_rms_norm(_dot(h, w_ref[:, :aw]), qg_ref[...], q_scale).astype(BF16)
    qkv_ref[:, aw:2 * aw] = _half_rms_norm(_dot(h, w_ref[:, aw:2 * aw]), kg_ref[...], 1.0).astype(BF16)
    qkv_ref[:, 2 * aw:] = _dot(h, w_ref[:, 2 * aw:3 * aw]).astype(BF16)
    u_ref[...] = _dot(h, w_ref[:, 3 * aw:])


def _inproj(x2, mod4, norm_g, w_bf, qg, kg, seq, tm):
    t, d = x2.shape
    aw = w_bf.shape[1] // 4
    tiles_per_seq = seq // tm
    bidx = lambda i: i // tiles_per_seq
    qg2 = jnp.tile(qg, 2).reshape(1, LANES)
    kg2 = jnp.tile(kg, 2).reshape(1, LANES)
    head_dim = qg.shape[0]
    return pl.pallas_call(
        functools.partial(_inproj_kernel, q_scale=head_dim ** -0.5 * math.log2(math.e)),
        out_shape=(jax.ShapeDtypeStruct((t, 3 * aw), BF16), jax.ShapeDtypeStruct((t, aw), F32)),
        grid=(t // tm,),
        in_specs=[pl.BlockSpec((tm, d), lambda i: (i, 0)),
                  pl.BlockSpec((1, d), lambda i: (0, 0)),
                  pl.BlockSpec((None, None, 1, d), lambda i: (bidx(i), 0, 0, 0)),
                  pl.BlockSpec((None, None, 1, d), lambda i: (bidx(i), 1, 0, 0)),
                  _resident(w_bf.shape),
                  pl.BlockSpec((1, LANES), lambda i: (0, 0)),
                  pl.BlockSpec((1, LANES), lambda i: (0, 0))],
        out_specs=(pl.BlockSpec((tm, 3 * aw), lambda i: (i, 0)),
                   pl.BlockSpec((tm, aw), lambda i: (i, 0))),
        compiler_params=_cparams("arbitrary"),
        name="in_proj",
    )(x2, norm_g.reshape(1, d), mod4, mod4, w_bf, qg2, kg2)


ATTN_TQ = 512
ATTN_TK = 512
ATTN_ROWS = 64
ATTN_HEADS_PER_STEP = 1
NEG_BIG = -1e30
POS_SPLIT = 64


def _attn_kernel(lamv_ref, gsub_ref, coef_ref, pos_ref, q_ref, k_ref, v_ref, o_ref,
                 kaug_ref, vaug_ref, qaug_ref, s_ref, p_ref, m_ref, alpha_ref, acc_ref,
                 *, tq, lambda_init):
    i = pl.program_id(2)
    half = LANES // 2
    heads = range(ATTN_HEADS_PER_STEP)
    head_lanes = [slice(hd * LANES, (hd + 1) * LANES) for hd in heads]

    @pl.when(i == 0)
    def _():
        for hd in heads:
            kaug_ref[hd, :, :LANES] = k_ref[:, head_lanes[hd]]
            kaug_ref[hd, :, LANES:] = pos_ref[...]
            vaug_ref[hd, :, :LANES] = v_ref[:, head_lanes[hd]]
            vaug_ref[hd, :, LANES:] = jnp.ones((v_ref.shape[0], LANES), BF16)

    for hd in heads:
        q = q_ref[:, head_lanes[hd]]
        lane = lax.broadcasted_iota(jnp.int32, q.shape, 1)
        zero = jnp.zeros_like(q)
        qaug_ref[hd, 0, :, :LANES] = jnp.where(lane < half, q, zero)
        qaug_ref[hd, 1, :, :LANES] = jnp.where(lane >= half, q, zero)
        coef = jnp.broadcast_to(coef_ref[hd], q.shape)
        qaug_ref[hd, 0, :, LANES:] = coef
        qaug_ref[hd, 1, :, LANES:] = coef

    m_ref[...] = jnp.full(m_ref.shape, NEG_BIG, F32)

    @pl.when((pl.program_id(0) == 0) & (pl.program_id(1) == 0) & (i == 0))
    def _():
        acc_ref[...] = jnp.zeros(acc_ref.shape, F32)
        alpha_ref[...] = jnp.ones(alpha_ref.shape, F32)
        p_ref[...] = jnp.zeros(p_ref.shape, BF16)

    nsub = tq // ATTN_ROWS
    tk = s_ref.shape[-1]
    hk = tk // 2
    per_tile = tq // tk

    def scores(j, idx, part, hd):
        start = pl.multiple_of(j * tk + part * hk, hk)
        s_ref[hd, idx, :, part * hk:(part + 1) * hk] = _dot_nt(qaug_ref[hd, idx],
                                                               kaug_ref[hd, pl.ds(start, hk), :])

    def softmax(idx, mask_off, chunks, hd):
        for r in chunks:
            rows = slice(r * ATTN_ROWS, (r + 1) * ATTN_ROWS)
            s = s_ref[hd, idx, rows, :]
            if mask_off is not None:
                row = lax.broadcasted_iota(jnp.int32, s.shape, 0) + r * ATTN_ROWS
                col = lax.broadcasted_iota(jnp.int32, s.shape, 1) + mask_off
                s = jnp.where(col <= row, s, NEG_BIG)
            m_old = m_ref[hd, idx, rows, :]
            m_new = jnp.maximum(m_old, jnp.max(s, axis=-1, keepdims=True))
            alpha_ref[hd, idx, rows, :] = jnp.exp2(m_old - m_new)
            m_ref[hd, idx, rows, :] = m_new
            p_ref[hd, idx, rows, :] = jnp.exp2(s - jnp.tile(m_new, (1, tk // LANES))).astype(BF16)

    def values(j, idx, hd):
        start = pl.multiple_of(jnp.maximum(j, 0) * tk, tk)
        pv = _dot(p_ref[hd, idx], vaug_ref[hd, pl.ds(start, tk), :])
        acc_ref[hd, idx] = jnp.tile(alpha_ref[hd, idx], (1, 2)) * acc_ref[hd, idx] + pv

    def unit(j_s, idx_s, idx_sm, mask_off, j_v, idx_v):
        matmuls = [functools.partial(values, j_v, idx_v)]
        if j_s is not None:
            matmuls += [functools.partial(scores, j_s, idx_s, part) for part in range(2)]
        bounds = [round(k * nsub / len(matmuls)) for k in range(len(matmuls) + 1)]
        for k, matmul in enumerate(matmuls):
            for hd in heads:
                matmul(hd)
            for hd in heads:
                softmax(idx_sm, mask_off, range(bounds[k], bounds[k + 1]), hd)

    for part in range(2):
        for hd in heads:
            scores(0, 0, part, hd)

    def block(j):
        unit(j, 1, 0, None, j - 1, 1)
        unit(j + 1, 0, 1, None, j, 0)

    def four_blocks(jj, carry):
        for t in range(4):
            block(4 * jj + t)
        return carry

    nfull = i * per_tile
    lax.fori_loop(0, nfull // 4, four_blocks, 0)

    @pl.when(nfull % 4 >= 2)
    def _():
        block(nfull // 4 * 4)
        block(nfull // 4 * 4 + 1)

    @pl.when(nfull % 2 == 1)
    def _():
        block(nfull - 1)

    tail = [(nfull + t, idx, t * tk) for t in range(per_tile) for idx in range(2)]
    j_v, idx_v = nfull - 1, 1
    for n, (j_sm, idx_sm, off) in enumerate(tail):
        j_s, idx_s = tail[n + 1][:2] if n + 1 < len(tail) else (None, None)
        unit(j_s, idx_s, idx_sm, off, j_v, idx_v)
        j_v, idx_v = j_sm, idx_sm
    for hd in heads:
        values(j_v, idx_v, hd)

    lv = lamv_ref[...]
    lam = (jnp.exp(jnp.sum(lv[0:1] * lv[1:2], axis=-1, keepdims=True))
           - jnp.exp(jnp.sum(lv[2:3] * lv[3:4], axis=-1, keepdims=True)) + lambda_init)
    for hd in heads:
        a0 = acc_ref[hd, 0]
        a1 = acc_ref[hd, 1]
        o = a0[:, :LANES] / a0[:, LANES:] - lam * (a1[:, :LANES] / a1[:, LANES:])
        o = o * lax.rsqrt(jnp.mean(o * o, axis=-1, keepdims=True) + EPS)
        o_ref[:, head_lanes[hd]] = (o * gsub_ref[...] * (1.0 - lambda_init)).astype(BF16)


def _split_bf16(x, n):
    parts = []
    for _ in range(n):
        p = x.astype(BF16)
        parts.append(p)
        x = x - p.astype(F32)
    return parts


def _attention(qkv, lamv, gsub, bsz, seq, lambda_init, tq, tk):
    t = qkv.shape[0]
    aw = qkv.shape[1] // 3
    heads = aw // LANES
    hps = ATTN_HEADS_PER_STEP
    groups = heads // hps
    nq = seq // tq
    c = jnp.asarray([2.0 ** (-8.0 * (i + 1) / heads) * math.log2(math.e) for i in range(heads)], F32)
    cparts = _split_bf16(c, 3)
    coef = jnp.stack([p * POS_SPLIT for p in cparts] + cparts, axis=-1)
    coef = jnp.pad(coef, ((0, 0), (0, LANES - coef.shape[1]))).reshape(heads, 1, LANES)
    kpos = jnp.arange(seq, dtype=jnp.int32)
    hi = (kpos // POS_SPLIT).astype(BF16)
    lo = (kpos % POS_SPLIT).astype(BF16)
    pos = jnp.pad(jnp.stack([hi, hi, hi, lo, lo, lo], axis=-1), ((0, 0), (0, LANES - 6)))
    return pl.pallas_call(
        functools.partial(_attn_kernel, tq=tq, lambda_init=lambda_init),
        out_shape=jax.ShapeDtypeStruct((t, aw), BF16),
        grid=(bsz, groups, nq),
        in_specs=[pl.BlockSpec((4, LANES // 2), lambda b, h, i: (0, 0)),
                  pl.BlockSpec((1, LANES), lambda b, h, i: (0, 0)),
                  pl.BlockSpec((hps, 1, LANES), lambda b, h, i: (h, 0, 0)),
                  pl.BlockSpec((seq, LANES), lambda b, h, i: (0, 0)),
                  pl.BlockSpec((tq, hps * LANES), lambda b, h, i: (b * nq + i, h)),
                  pl.BlockSpec((seq, hps * LANES), lambda b, h, i: (b, groups + h)),
                  pl.BlockSpec((seq, hps * LANES), lambda b, h, i: (b, 2 * groups + h))],
        out_specs=pl.BlockSpec((tq, hps * LANES), lambda b, h, i: (b * nq + i, h)),
        scratch_shapes=[pltpu.VMEM((hps, seq, 2 * LANES), BF16), pltpu.VMEM((hps, seq, 2 * LANES), BF16),
                        pltpu.VMEM((hps, 2, tq, 2 * LANES), BF16),
                        pltpu.VMEM((hps, 2, tq, tk), F32), pltpu.VMEM((hps, 2, tq, tk), BF16),
                        pltpu.VMEM((hps, 2, tq, LANES), F32), pltpu.VMEM((hps, 2, tq, LANES), F32),
                        pltpu.VMEM((hps, 2, tq, 2 * LANES), F32)],
        compiler_params=_cparams("arbitrary", "arbitrary", "arbitrary"),
        name="diff_attn",
    )(lamv, gsub.reshape(1, LANES), coef, pos, qkv, qkv, qkv)


def _ssm_matrix_kernel(c_re_ref, c_im_ref, bt_re_ref, bt_im_ref, pw_re_ref, pw_im_ref,
                       tm_ref, bd_ref, cd_ref):
    q = SSM_CHUNK
    nrow, p = c_re_ref.shape
    width = pw_re_ref.shape[1]
    gb = width // p
    row_g = lax.broadcasted_iota(jnp.int32, (nrow, width), 0) // (nrow // gb)
    col_g = lax.broadcasted_iota(jnp.int32, (nrow, width), 1) // p
    same_group = row_g == col_g

    def expand(ref):
        return jnp.where(same_group, jnp.tile(ref[...], (1, gb)), 0.0)

    def times_power(xr, xi, n):
        pr = pw_re_ref[n:n + 1, :]
        pi = pw_im_ref[n:n + 1, :]
        return xr * pr - xi * pi, xr * pi + xi * pr

    def dot_nt_f32(a, b):
        a_hi, b_hi = a.astype(BF16), b.astype(BF16)
        a_lo = (a - a_hi.astype(F32)).astype(BF16)
        b_lo = (b - b_hi.astype(F32)).astype(BF16)
        return _dot_nt(a_hi, b_hi) + _dot_nt(a_hi, b_lo) + _dot_nt(a_lo, b_hi)

    c_re, c_im = expand(c_re_ref), expand(c_im_ref)
    bt_re, bt_im = expand(bt_re_ref), expand(bt_im_ref)

    tm_ref[...] = jnp.zeros(tm_ref.shape, BF16)
    for tau in range(q):
        ca_re, ca_im = times_power(c_re, c_im, tau)
        blk = (dot_nt_f32(bt_re, ca_re) - dot_nt_f32(bt_im, ca_im)).astype(BF16)
        for jp in range(q - tau):
            tm_ref[jp * LANES:(jp + 1) * LANES, (jp + tau) * LANES:(jp + tau + 1) * LANES] = blk
    for jp in range(q):
        e_re, e_im = times_power(bt_re, bt_im, q - 1 - jp)
        bd_ref[jp * LANES:(jp + 1) * LANES, :width] = e_re.astype(BF16)
        bd_ref[jp * LANES:(jp + 1) * LANES, width:] = e_im.astype(BF16)
    for j in range(q):
        f_re, f_im = times_power(c_re, c_im, j + 1)
        cd_ref[:width, j * LANES:(j + 1) * LANES] = f_re.T.astype(BF16)
        cd_ref[width:, j * LANES:(j + 1) * LANES] = (-f_im).T.astype(BF16)


def _ssm_matrices(a_re, a_im, log_dt, b_re, b_im, c_re, c_im, nk):
    g, p = a_re.shape
    cch = b_re.shape[-1]
    q = SSM_CHUNK
    gb = LANES // cch
    nb = g // gb
    dt = jnp.exp(log_dt.astype(F32))[:, None]
    ar, ai = a_re.astype(F32), a_im.astype(F32)
    mag = jnp.exp(dt * ar)
    abar_re, abar_im = mag * jnp.cos(dt * ai), mag * jnp.sin(dt * ai)
    den = ar * ar + ai * ai
    nr, ni = abar_re - 1.0, abar_im
    coef_re = (nr * ar + ni * ai) / den
    coef_im = (ni * ar - nr * ai) / den
    br, bi = b_re.astype(F32), b_im.astype(F32)
    bbar_re = coef_re[..., None] * br - coef_im[..., None] * bi
    bbar_im = coef_re[..., None] * bi + coef_im[..., None] * br

    def power(n):
        n = jnp.asarray(n, F32)[:, None, None]
        m = jnp.exp(n * (dt * ar)[None])
        return m * jnp.cos(n * (dt * ai)[None]), m * jnp.sin(n * (dt * ai)[None])

    rows_c = lambda x: x.reshape(g * cch, p)
    c2 = [rows_c(c_re.astype(F32)), rows_c(c_im.astype(F32))]
    bt2 = [rows_c(bbar_re.transpose(0, 2, 1)), rows_c(bbar_im.transpose(0, 2, 1))]
    pw = [x.reshape(q + 1, g * p) for x in power(range(q + 1))]
    width = gb * p
    side = q * LANES
    par = pl.BlockSpec((LANES, p), lambda i: (i, 0))
    pws = pl.BlockSpec((q + 1, width), lambda i: (0, i))
    mat = lambda r, c: pl.BlockSpec((None, r, c), lambda i: (i, 0, 0))
    tmat, bd, cd = pl.pallas_call(
        _ssm_matrix_kernel,
        out_shape=(jax.ShapeDtypeStruct((nb, side, side), BF16),
                   jax.ShapeDtypeStruct((nb, side, 2 * width), BF16),
                   jax.ShapeDtypeStruct((nb, 2 * width, side), BF16)),
        grid=(nb,),
        in_specs=[par, par, par, par, pws, pws],
        out_specs=(mat(side, side), mat(side, 2 * width), mat(2 * width, side)),
        compiler_params=_cparams("arbitrary"),
        name="s5_matrices",
    )(c2[0], c2[1], bt2[0], bt2[1], pw[0], pw[1])

    def lanes(x):
        return x.reshape(x.shape[0], nb, width).transpose(1, 0, 2)

    aq = [lanes(x) for x in power([q])]
    aseg = [lanes(x) for x in power([q * nk])]
    return tmat, bd, cd, aq, aseg


def _ssm_kernel(u_ref, tm_ref, bd_ref, cd_ref, aqr_ref, aqi_ref, asr_ref, asi_ref,
                d_ref, y_ref, z_ref, yi_ref, sre_ref, sim_ref, xre_ref, xim_ref, xn_ref, *, bsz, nk):
    q = SSM_CHUNK
    rows = z_ref.shape[0]
    nslab = sre_ref.shape[0]
    half = nslab * LANES
    nseq = rows // nk
    nseg = nseq // bsz
    step_n = MXU_WIDTH

    for j in range(q):
        z_ref[:, j * LANES:(j + 1) * LANES] = u_ref[pl.ds(j, rows, stride=q), :].astype(BF16)
    z = z_ref[...]
    for c in range(0, q * LANES, step_n):
        yi_ref[:, c:c + step_n] = _dot(z, tm_ref[:, c:c + step_n])
    for c in range(0, 2 * half, step_n):
        s = _dot(z, bd_ref[:, c:c + step_n])
        dst = sre_ref if c < half else sim_ref
        for hh in range(step_n // LANES):
            slab = (c % half) // LANES + hh
            for sq in range(nseq):
                dst[slab, pl.ds(sq, nk, stride=nseq), :] = s[sq * nk:(sq + 1) * nk, hh * LANES:(hh + 1) * LANES]

    ar = [jnp.broadcast_to(aqr_ref[:, s * LANES:(s + 1) * LANES], (nseq, LANES)) for s in range(nslab)]
    ai = [jnp.broadcast_to(aqi_ref[:, s * LANES:(s + 1) * LANES], (nseq, LANES)) for s in range(nslab)]

    def step(k, carry):
        xr, xi = carry
        base = pl.multiple_of(k * nseq, nseq)
        nr, ni = [], []
        for s in range(nslab):
            xre_ref[s, pl.ds(base, nseq), :] = xr[s]
            xim_ref[s, pl.ds(base, nseq), :] = xi[s]
            nr.append(ar[s] * xr[s] - ai[s] * xi[s] + sre_ref[s, pl.ds(base, nseq), :])
            ni.append(ar[s] * xi[s] + ai[s] * xr[s] + sim_ref[s, pl.ds(base, nseq), :])
        return tuple(nr), tuple(ni)

    zero = tuple(jnp.zeros((nseq, LANES), F32) for _ in range(nslab))
    er, ei = lax.fori_loop(0, nk, step, (zero, zero), unroll=2)

    cr, ci = [], []
    for s in range(nslab):
        asr = asr_ref[:, s * LANES:(s + 1) * LANES]
        asi = asi_ref[:, s * LANES:(s + 1) * LANES]
        rows_r, rows_i = [], []
        for b in range(bsz):
            cre = jnp.zeros((1, LANES), F32)
            cim = jnp.zeros((1, LANES), F32)
            for sg in range(nseg):
                rows_r.append(cre)
                rows_i.append(cim)
                sq = b * nseg + sg
                cre, cim = (er[s][sq:sq + 1] + asr * cre - asi * cim,
                            ei[s][sq:sq + 1] + asr * cim + asi * cre)
        cr.append(jnp.concatenate(rows_r, axis=0))
        ci.append(jnp.concatenate(rows_i, axis=0))

    def fix(k, carry):
        wr, wi = carry
        base = pl.multiple_of(k * nseq, nseq)
        nr, ni = [], []
        for s in range(nslab):
            xre_ref[s, pl.ds(base, nseq), :] += wr[s]
            xim_ref[s, pl.ds(base, nseq), :] += wi[s]
            nr.append(ar[s] * wr[s] - ai[s] * wi[s])
            ni.append(ar[s] * wi[s] + ai[s] * wr[s])
        return tuple(nr), tuple(ni)

    lax.fori_loop(0, nk, fix, (tuple(cr), tuple(ci)), unroll=2)

    for s in range(nslab):
        for sq in range(nseq):
            rs = slice(sq * nk, (sq + 1) * nk)
            xn_ref[rs, s * LANES:(s + 1) * LANES] = xre_ref[s, pl.ds(sq, nk, stride=nseq), :].astype(BF16)
            xn_ref[rs, half + s * LANES:half + (s + 1) * LANES] = (
                xim_ref[s, pl.ds(sq, nk, stride=nseq), :].astype(BF16))
    xn = xn_ref[...]
    d = d_ref[...]
    for c in range(0, q * LANES, step_n):
        yy = yi_ref[:, c:c + step_n] + _dot(xn, cd_ref[:, c:c + step_n])
        for hh in range(step_n // LANES):
            j = c // LANES + hh
            y_ref[pl.ds(j, rows, stride=q), :] = (yy[:, hh * LANES:(hh + 1) * LANES]
                                                  + d * u_ref[pl.ds(j, rows, stride=q), :])


def _ssm(u2, a_re, a_im, log_dt, b_re, b_im, c_re, c_im, d_skip, bsz):
    t, sw = u2.shape
    q = SSM_CHUNK
    rows = t // q
    assert SUBLANES % bsz == 0 and rows % SUBLANES == 0
    nk = rows // SUBLANES
    tmat, bd, cd, aq, aseg = _ssm_matrices(a_re, a_im, log_dt, b_re, b_im, c_re, c_im, nk)
    nb, _, st = bd.shape
    half = st // 2
    nslab = half // LANES
    mat = lambda shape: pl.BlockSpec((None,) + shape, lambda i: (i, 0, 0))
    return pl.pallas_call(
        functools.partial(_ssm_kernel, bsz=bsz, nk=nk),
        out_shape=jax.ShapeDtypeStruct((t, sw), F32),
        grid=(nb,),
        in_specs=[pl.BlockSpec((t, LANES), lambda i: (0, i)),
                  mat((q * LANES, q * LANES)), mat((q * LANES, st)), mat((st, q * LANES)),
                  mat((1, half)), mat((1, half)), mat((1, half)), mat((1, half)),
                  pl.BlockSpec((1, LANES), lambda i: (0, i))],
        out_specs=pl.BlockSpec((t, LANES), lambda i: (0, i)),
        scratch_shapes=[pltpu.VMEM((rows, q * LANES), BF16), pltpu.VMEM((rows, q * LANES), F32)]
                       + [pltpu.VMEM((nslab, rows, LANES), F32) for _ in range(4)]
                       + [pltpu.VMEM((rows, st), BF16)],
        compiler_params=_cparams("arbitrary"),
        name="s5_scan",
    )(u2, tmat, bd, cd, aq[0], aq[1], aseg[0], aseg[1],
      d_skip.astype(F32).reshape(1, sw))


def _mix_kernel(x_ref, attn_ref, y_ref, wglu_ref, bglu_ref, wout_ref, gate_ref,
                g_ref, shift_ref, scale_ref, x1_ref, h2_ref):
    aw = attn_ref.shape[1]
    yg = jax.nn.gelu(y_ref[...])
    z = _dot(yg.astype(BF16), wglu_ref[...]) + bglu_ref[...]
    yy = yg * jax.nn.sigmoid(z)
    mix = _dot(attn_ref[...], wout_ref[:aw, :]) + _dot(yy.astype(BF16), wout_ref[aw:, :])
    x1 = x_ref[...] + gate_ref[...] * mix
    x1_ref[...] = x1
    h2_ref[...] = _norm_modulate(x1, g_ref[...], shift_ref[...], scale_ref[...]).astype(BF16)


def _mix(x2, attn, y, wglu_bf, bglu, wout_bf, mod4, norm_g, seq, tm):
    t, d = x2.shape
    aw = attn.shape[1]
    tiles_per_seq = seq // tm
    bidx = lambda i: i // tiles_per_seq
    modspec = lambda k: pl.BlockSpec((None, None, 1, d), lambda i: (bidx(i), k, 0, 0))
    return pl.pallas_call(
        _mix_kernel,
        out_shape=(jax.ShapeDtypeStruct((t, d), F32), jax.ShapeDtypeStruct((t, d), BF16)),
        grid=(t // tm,),
        in_specs=[pl.BlockSpec((tm, d), lambda i: (i, 0)),
                  pl.BlockSpec((tm, aw), lambda i: (i, 0)),
                  pl.BlockSpec((tm, aw), lambda i: (i, 0)),
                  _resident(wglu_bf.shape),
                  pl.BlockSpec((1, aw), lambda i: (0, 0)),
                  _resident(wout_bf.shape),
                  modspec(2),
                  pl.BlockSpec((1, d), lambda i: (0, 0)),
                  modspec(3), modspec(4)],
        out_specs=(pl.BlockSpec((tm, d), lambda i: (i, 0)), pl.BlockSpec((tm, d), lambda i: (i, 0))),
        compiler_params=_cparams("arbitrary"),
        name="mix_out_proj",
    )(x2, attn, y, wglu_bf, bglu.reshape(1, aw), wout_bf, mod4, norm_g.reshape(1, d), mod4, mod4)


def _ffn_kernel(h_ref, halo_ref, wa_ref, wg_ref, cw_ref, cb_ref, wd_ref, x1_ref, gate_ref,
                o_ref, a0_ref, a1_ref, g0_ref, g1_ref, act0_ref, act1_ref, acc_ref,
                *, tiles_per_seq, nf, units):
    n = pl.program_id(0)
    slot = n % 2
    f_down = jnp.maximum(n - 2, 0) % nf
    tm = h_ref.shape[0]

    @pl.when(n == 0)
    def _():
        a1_ref[...] = jnp.zeros(a1_ref.shape, F32)
        g1_ref[...] = jnp.zeros(g1_ref.shape, F32)
        act0_ref[...] = jnp.zeros(act0_ref.shape, BF16)

    @pl.when(f_down == 0)
    def _():
        acc_ref[...] = jnp.zeros(acc_ref.shape, F32)

    row_tile = jnp.minimum(n, units - 1) // nf

    def stages(a_rd, g_rd, a_wr, g_wr, act_rd, act_wr):
        halo = jnp.where(row_tile % tiles_per_seq == 0, jnp.zeros_like(halo_ref[...]), halo_ref[...])
        h = h_ref[...]
        h_ext = jnp.concatenate([halo, h], axis=0)
        act = act_rd[...]
        cw = cw_ref[...]
        cb = cb_ref[...]
        tf = a_wr.shape[1]
        d = acc_ref.shape[1]

        def down(c):
            cols = slice(c * (d // FFN_PARTS), (c + 1) * (d // FFN_PARTS))
            acc_ref[:, cols] += _dot(act, wd_ref[:, cols])

        def up(c, lhs, w_ref, dst):
            cols = slice(c * (2 * tf // FFN_PARTS), (c + 1) * (2 * tf // FFN_PARTS))
            dst[:, cols] = _dot(lhs, w_ref[:, cols])

        def conv_gelu(r):
            nrow = tm // (2 * FFN_PARTS)
            r0 = CONV_HALO + r * nrow
            conv = (cw[2:3] * a_rd[r0:r0 + nrow, :] + cw[1:2] * a_rd[r0 - 1:r0 - 1 + nrow, :]
                    + cw[0:1] * a_rd[r0 - 2:r0 - 2 + nrow, :] + cb)
            act_wr[r * nrow:(r + 1) * nrow, :] = (
                jax.nn.gelu(conv) * g_rd[r * nrow:(r + 1) * nrow, :]).astype(BF16)

        mxu_work = ([functools.partial(down, c) for c in range(2)]
                    + [functools.partial(up, 0, h_ext, wa_ref, a_wr), functools.partial(down, 2),
                       functools.partial(up, 0, h, wg_ref, g_wr), functools.partial(down, 3),
                       functools.partial(up, 1, h_ext, wa_ref, a_wr), functools.partial(up, 1, h, wg_ref, g_wr)])
        for r, matmul in enumerate(mxu_work):
            matmul()
            conv_gelu(r)

    @pl.when(slot == 0)
    def _():
        stages(a1_ref, g1_ref, a0_ref, g0_ref, act0_ref, act1_ref)

    @pl.when(slot == 1)
    def _():
        stages(a0_ref, g0_ref, a1_ref, g1_ref, act1_ref, act0_ref)

    @pl.when((f_down == nf - 1) & (n > 1))
    def _():
        o_ref[...] = x1_ref[...] + gate_ref[...] * acc_ref[...]


def _ffn(h2, x1, wup_bf, conv_w, conv_b, wdown_bf, mod4, seq, tm, tf):
    t, d = h2.shape
    dff = wdown_bf.shape[0]
    nf = dff // tf
    tiles_per_seq = seq // tm
    halo_blocks = tm // CONV_HALO
    units = (t // tm) * nf
    cur_tile = lambda n: jnp.minimum(n, units - 1) // nf
    cur_blk = lambda n: jnp.minimum(n, units - 1) % nf
    conv_blk = lambda n: jnp.clip(n - 1, 0, units - 1) % nf
    down_tile = lambda n: jnp.maximum(n - 2, 0) // nf
    down_blk = lambda n: jnp.maximum(n - 2, 0) % nf
    return pl.pallas_call(
        functools.partial(_ffn_kernel, tiles_per_seq=tiles_per_seq, nf=nf, units=units),
        out_shape=jax.ShapeDtypeStruct((t, d), F32),
        grid=(units + 2,),
        in_specs=[pl.BlockSpec((tm, d), lambda n: (cur_tile(n), 0)),
                  pl.BlockSpec((CONV_HALO, d), lambda n: (jnp.maximum(cur_tile(n) * halo_blocks - 1, 0), 0)),
                  pl.BlockSpec((d, tf), lambda n: (0, cur_blk(n))),
                  pl.BlockSpec((d, tf), lambda n: (0, nf + cur_blk(n))),
                  pl.BlockSpec((conv_w.shape[0], tf), lambda n: (0, conv_blk(n))),
                  pl.BlockSpec((1, tf), lambda n: (0, conv_blk(n))),
                  pl.BlockSpec((tf, d), lambda n: (down_blk(n), 0)),
                  pl.BlockSpec((tm, d), lambda n: (down_tile(n), 0)),
                  pl.BlockSpec((None, None, 1, d), lambda n: (down_tile(n) // tiles_per_seq, 5, 0, 0))],
        out_specs=pl.BlockSpec((tm, d), lambda n: (down_tile(n), 0)),
        scratch_shapes=[pltpu.VMEM((tm + CONV_HALO, tf), F32), pltpu.VMEM((tm + CONV_HALO, tf), F32),
                        pltpu.VMEM((tm, tf), F32), pltpu.VMEM((tm, tf), F32),
                        pltpu.VMEM((tm, tf), BF16), pltpu.VMEM((tm, tf), BF16),
                        pltpu.VMEM((tm, d), F32)],
        compiler_params=_cparams("arbitrary"),
        name="conv_ffn",
    )(h2, h2, wup_bf, wup_bf, conv_w, conv_b.reshape(1, dff), wdown_bf, x1, mod4)


def kernel(x, c, w_ada, b_ada, norm1_g, norm2_g, w_in, q_norm_g, k_norm_g, lambda_q1, lambda_k1,
           lambda_q2, lambda_k2, attn_sub_norm_g, ssm_a_re, ssm_a_im, ssm_log_dt, ssm_b_re, ssm_b_im,
           ssm_c_re, ssm_c_im, ssm_d, w_glu, b_glu, w_out, w_up, conv_w, conv_b, w_down):
    bsz, seq, d = x.shape
    depth = w_ada.shape[0]
    tm = min(ROW_TILE, seq)
    x2 = x.reshape(bsz * seq, d)
    for l in range(depth):
        lambda_init = 0.8 - 0.6 * math.exp(-0.3 * l)
        mod = _ada(c, w_ada[l], b_ada[l])
        mod4 = mod.reshape(bsz, 6, 1, d)
        qkv, u = _inproj(x2, mod4, norm1_g[l], w_in[l].astype(BF16), q_norm_g[l], k_norm_g[l], seq, tm)
        lamv = jnp.stack([lambda_q1[l], lambda_k1[l], lambda_q2[l], lambda_k2[l]]).astype(F32)
        attn = _attention(qkv, lamv, attn_sub_norm_g[l], bsz, seq, lambda_init,
                          min(ATTN_TQ, seq), min(ATTN_TK, seq))
        y = _ssm(u, ssm_a_re[l], ssm_a_im[l], ssm_log_dt[l], ssm_b_re[l], ssm_b_im[l],
                 ssm_c_re[l], ssm_c_im[l], ssm_d[l], bsz)
        x1, h2 = _mix(x2, attn, y, w_glu[l].astype(BF16), b_glu[l], w_out[l].astype(BF16), mod4,
                      norm2_g[l], seq, tm)
        x2 = _ffn(h2, x1, w_up[l].astype(BF16), conv_w[l], conv_b[l], w_down[l].astype(BF16), mod4,
                  seq, tm, FFN_TF)
    return x2.reshape(bsz, seq, d)
```
